```python
import jax, jax.numpy as jnp
from jax import lax
import numpy as np

D_MODEL = 2048
BATCH = 2
SEQ = 16384
DEPTH = 1

MIX_WIDTH = D_MODEL
SB_HEADS = 8
SB_HEAD_DIM = 128
SB_WIDTH = SB_HEADS * SB_HEAD_DIM
SB_BLOCK = 128
SSD_HEADS = 16
SSD_HEAD_DIM = 64
SSD_WIDTH = SSD_HEADS * SSD_HEAD_DIM
SSD_GROUPS = 2
SSD_STATE = 128
SSD_CONV = 4
SSD_CHUNK = 128
SSD_CONV_DIM = SSD_WIDTH + 2 * SSD_GROUPS * SSD_STATE
IN_PROJ_DIM = 3 * SB_WIDTH + SSD_WIDTH + SSD_CONV_DIM + SSD_HEADS
PEER_HEADS = 8
PEER_N_KEYS = 128
PEER_N_EXPERTS = PEER_N_KEYS * PEER_N_KEYS
PEER_KEY_DIM = 256
PEER_HALF = PEER_KEY_DIM // 2
PEER_TOPK = 16
PEER_TOKEN_BLOCK = 128
RMS_EPS = 1e-6

kernel_name = "hymba_sb_ssd_peer_block"


def rms_norm(x, gain):
    xf = x.astype(jnp.float32)
    inv = lax.rsqrt(jnp.mean(xf * xf, axis=-1, keepdims=True) + RMS_EPS)
    return (xf * inv).astype(x.dtype) * gain


def stick_breaking_attention(q, k, v):
    b, s, h, dh = q.shape
    nb = s // SB_BLOCK
    scale = dh ** -0.5
    kf = k.astype(jnp.float32)
    vf = v.astype(jnp.float32)
    key_pos = jnp.arange(s)
    q_blocks = q.reshape(b, nb, SB_BLOCK, h, dh).transpose(1, 0, 2, 3, 4)

    def block(args):
        qb, bi = args
        q_pos = bi * SB_BLOCK + jnp.arange(SB_BLOCK)
        logits = jnp.einsum('bqhd,bkhd->bhqk', qb.astype(jnp.float32), kf) * scale
        causal = key_pos[None, :] < q_pos[:, None]
        log_beta = jax.nn.log_sigmoid(logits)
        log_keep = jnp.where(causal, jax.nn.log_sigmoid(-logits), 0.0)
        later = lax.cumsum(log_keep, axis=3, reverse=True) - log_keep
        weights = jnp.where(causal, jnp.exp(jnp.where(causal, log_beta + later, 0.0)), 0.0)
        return jnp.einsum('bhqk,bkhd->bqhd', weights, vf)

    out = lax.map(block, (q_blocks, jnp.arange(nb)))
    return out.transpose(1, 0, 2, 3, 4).reshape(b, s, h, dh).astype(q.dtype)


def ssd_scan(x, dt, a, bmat, cmat):
    f32 = jnp.float32
    b, s, h, p = x.shape
    g, n = bmat.shape[-2], bmat.shape[-1]
    hg = h // g
    nc, l = s // SSD_CHUNK, SSD_CHUNK
    xf = (x.astype(f32) * dt[..., None]).reshape(b, nc, l, g, hg, p)
    a_dt = (dt * a).reshape(b, nc, l, g, hg).transpose(0, 3, 4, 1, 2)
    bf = bmat.astype(f32).reshape(b, nc, l, g, n)
    cf = cmat.astype(f32).reshape(b, nc, l, g, n)
    a_cs = jnp.cumsum(a_dt, axis=-1)
    seg = a_cs[..., :, None] - a_cs[..., None, :]
    tri = jnp.tril(jnp.ones((l, l), dtype=bool))
    decay = jnp.where(tri, jnp.exp(jnp.where(tri, seg, 0.0)), 0.0).transpose(0, 3, 1, 2, 4, 5)
    cb = jnp.einsum('bclgn,bcsgn->bcgls', cf, bf)
    scores = cb[:, :, :, None] * decay
    y_diag = jnp.einsum('bcghls,bcsghp->bclghp', scores, xf)
    decay_states = jnp.exp(a_cs[..., -1:] - a_cs).transpose(0, 3, 4, 1, 2)
    states = jnp.einsum('bclgn,bclghp->bcghpn', bf, xf * decay_states[..., None])
    chunk_decay = jnp.exp(a_cs[..., -1]).transpose(3, 0, 1, 2)

    def step(carry, inp):
        st, dec = inp
        return carry * dec[..., None, None] + st, carry

    init = jnp.zeros((b, g, hg, p, n), f32)
    _, prev = lax.scan(step, init, (states.transpose(1, 0, 2, 3, 4, 5), chunk_decay))
    prev = prev.transpose(1, 0, 2, 3, 4, 5)
    state_decay_out = jnp.exp(a_cs).transpose(0, 3, 4, 1, 2)
    y_off = jnp.einsum('bclgn,bcghpn->bclghp', cf, prev) * state_decay_out[..., None]
    return (y_diag + y_off).reshape(b, s, h, p)


def hybrid_mixer(xn, w_in, sb_q_norm, sb_k_norm, conv_w, conv_b, dt_bias, a_log, d_skip,
                 sb_out_norm, ssd_out_norm, w_out):
    b, s, _ = xn.shape
    proj = xn @ w_in
    o1 = SB_WIDTH
    o2 = 2 * SB_WIDTH
    o3 = 3 * SB_WIDTH
    o4 = o3 + SSD_WIDTH
    o5 = o4 + SSD_CONV_DIM
    q, k, v, z, xbc, dt_raw = jnp.split(proj, [o1, o2, o3, o4, o5], axis=-1)
    q = rms_norm(q.reshape(b, s, SB_HEADS, SB_HEAD_DIM), sb_q_norm)
    k = rms_norm(k.reshape(b, s, SB_HEADS, SB_HEAD_DIM), sb_k_norm)
    v = v.reshape(b, s, SB_HEADS, SB_HEAD_DIM)
    attn = stick_breaking_attention(q, k, v).reshape(b, s, SB_WIDTH)
    attn = rms_norm(attn, sb_out_norm)
    xbc = lax.conv_general_dilated(xbc, conv_w, window_strides=(1,), padding=[(SSD_CONV - 1, 0)],
                                   dimension_numbers=('NWC', 'WIO', 'NWC'),
                                   feature_group_count=SSD_CONV_DIM) + conv_b
    xbc = jax.nn.silu(xbc)
    xs, bm, cm = jnp.split(xbc, [SSD_WIDTH, SSD_WIDTH + SSD_GROUPS * SSD_STATE], axis=-1)
    xs = xs.reshape(b, s, SSD_HEADS, SSD_HEAD_DIM)
    bm = bm.reshape(b, s, SSD_GROUPS, SSD_STATE)
    cm = cm.reshape(b, s, SSD_GROUPS, SSD_STATE)
    dt = jax.nn.softplus(dt_raw.astype(jnp.float32) + dt_bias.astype(jnp.float32))
    a = -jnp.exp(a_log.astype(jnp.float32))
    y = ssd_scan(xs, dt, a, bm, cm) + d_skip[:, None] * xs
    y = y.reshape(b, s, SSD_WIDTH).astype(xn.dtype) * jax.nn.silu(z)
    y = rms_norm(y.reshape(b, s, SSD_GROUPS, SSD_WIDTH // SSD_GROUPS),
                 ssd_out_norm.reshape(SSD_GROUPS, SSD_WIDTH // SSD_GROUPS)).reshape(b, s, SSD_WIDTH)
    return jnp.concatenate([attn, y], axis=-1) @ w_out


def peer_ffn(hn, w_query, sub_keys, expert_down, expert_up):
    b, s, d = hn.shape
    t = b * s
    xt = hn.reshape(t, d)
    q = (xt @ w_query).reshape(t, PEER_HEADS, 2, PEER_HALF)
    scores = jnp.einsum('thcd,hckd->thck', q.astype(jnp.float32), sub_keys.astype(jnp.float32))
    top_s, top_i = lax.top_k(scores, PEER_TOPK)
    cand_s = (top_s[:, :, 0, :, None] + top_s[:, :, 1, None, :]).reshape(t, PEER_HEADS, PEER_TOPK * PEER_TOPK)
    cand_i = (top_i[:, :, 0, :, None] * PEER_N_KEYS + top_i[:, :, 1, None, :]).reshape(t, PEER_HEADS, PEER_TOPK * PEER_TOPK)
    best_s, best_pos = lax.top_k(cand_s, PEER_TOPK)
    expert_idx = jnp.take_along_axis(cand_i, best_pos, axis=-1)
    gates = jax.nn.softmax(best_s, axis=-1)
    nblk = t // PEER_TOKEN_BLOCK

    def block(args):
        xb, idx, gb = args
        u = expert_down[idx]
        act = jax.nn.gelu(jnp.einsum('thkd,td->thk', u, xb), approximate=False)
        vv = expert_up[idx]
        return jnp.einsum('thk,thkd->td', (gb * act).astype(vv.dtype), vv)

    out = lax.map(block, (xt.reshape(nblk, PEER_TOKEN_BLOCK, d),
                          expert_idx.reshape(nblk, PEER_TOKEN_BLOCK, PEER_HEADS, PEER_TOPK),
                          gates.reshape(nblk, PEER_TOKEN_BLOCK, PEER_HEADS, PEER_TOPK)))
    return out.reshape(b, s, d).astype(hn.dtype)


def setup_inputs(seed: int = 0) -> dict:
    key = jax.random.key(seed)
    ks = jax.random.split(key, 20)
    f32 = jnp.float32
    nrm = lambda k, shape, sc: jax.random.normal(k, shape, f32) * sc
    gain = lambda k, shape: 1.0 + 0.02 * jax.random.normal(k, shape, f32)
    dt0 = jnp.exp(jax.random.uniform(ks[8], (DEPTH, SSD_HEADS), f32, float(np.log(1e-3)), float(np.log(1e-1))))
    return {
        "x": jax.random.normal(ks[0], (BATCH, SEQ, D_MODEL), f32),
        "attn_norm": gain(ks[1], (DEPTH, D_MODEL)),
        "w_in": nrm(ks[2], (DEPTH, D_MODEL, IN_PROJ_DIM), D_MODEL ** -0.5),
        "sb_q_norm": gain(ks[3], (DEPTH, SB_HEAD_DIM)),
        "sb_k_norm": gain(ks[4], (DEPTH, SB_HEAD_DIM)),
        "conv_w": nrm(ks[5], (DEPTH, SSD_CONV, 1, SSD_CONV_DIM), SSD_CONV ** -0.5),
        "conv_b": nrm(ks[6], (DEPTH, SSD_CONV_DIM), 0.02),
        "dt_bias": jnp.log(jnp.expm1(dt0)),
        "a_log": jnp.log(jax.random.uniform(ks[9], (DEPTH, SSD_HEADS), f32, 1.0, 16.0)),
        "d_skip": gain(ks[10], (DEPTH, SSD_HEADS)),
        "sb_out_norm": gain(ks[11], (DEPTH, SB_WIDTH)),
        "ssd_out_norm": gain(ks[12], (DEPTH, SSD_WIDTH)),
        "w_out": nrm(ks[13], (DEPTH, MIX_WIDTH, D_MODEL), MIX_WIDTH ** -0.5),
        "ffn_norm": gain(ks[14], (DEPTH, D_MODEL)),
        "peer_query": nrm(ks[15], (DEPTH, D_MODEL, PEER_HEADS * PEER_KEY_DIM), D_MODEL ** -0.5),
        "peer_sub_keys": nrm(ks[16], (DEPTH, PEER_HEADS, 2, PEER_N_KEYS, PEER_HALF), PEER_HALF ** -0.5),
        "peer_down": nrm(ks[17], (DEPTH, PEER_N_EXPERTS, D_MODEL), D_MODEL ** -0.5),
        "peer_up": nrm(ks[18], (DEPTH, PEER_N_EXPERTS, D_MODEL), (PEER_HEADS * PEER_TOPK) ** -0.5),
    }


def reference(x, attn_norm, w_in, sb_q_norm, sb_k_norm, conv_w, conv_b, dt_bias, a_log, d_skip,
              sb_out_norm, ssd_out_norm, w_out, ffn_norm, peer_query, peer_sub_keys, peer_down, peer_up):
    h = x
    for layer in range(DEPTH):
        xn = rms_norm(h, attn_norm[layer])
        h = h + hybrid_mixer(xn, w_in[layer], sb_q_norm[layer], sb_k_norm[layer], conv_w[layer],
                             conv_b[layer], dt_bias[layer], a_log[layer], d_skip[layer],
                             sb_out_norm[layer], ssd_out_norm[layer], w_out[layer])
        hn = rms_norm(h, ffn_norm[layer])
        h = h + peer_ffn(hn, peer_query[layer], peer_sub_keys[layer], peer_down[layer], peer_up[layer])
    return h
```

```python
import functools
import math

import jax
import jax.numpy as jnp
from jax import lax
from jax.experimental import pallas as pl
from jax.experimental.pallas import tpu as pltpu

F32 = jnp.float32
BF16 = jnp.bfloat16

RMS_EPS = 1e-6
SB_HEAD_DIM = 128
SSD_HEAD_DIM = 64
SSD_GROUPS = 2
SSD_STATE = 128
SSD_CONV = 4
SSD_CHUNK = 128
PEER_TOPK = 16
LANE = 128
SUBLANE = 8
VMEM_LIMIT = 56 * 1024 * 1024

EXP_ZERO_BOUND = -105.0

NEG_INF = float("-inf")


def _nt_dot(a, b, precision=None):
    return lax.dot_general(a, b, (((1,), (1,)), ((), ())), precision=precision,
                           preferred_element_type=F32)


def _params(*sem):
    return pltpu.CompilerParams(dimension_semantics=sem, vmem_limit_bytes=VMEM_LIMIT)


def _in_proj_kernel(x_ref, g_ref, w_ref, o_ref, xn_ref):
    @pl.when(pl.program_id(1) == 0)
    def _():
        x = x_ref[...]
        inv = lax.rsqrt(jnp.mean(x * x, axis=-1, keepdims=True) + RMS_EPS)
        xn_ref[...] = ((x * inv) * g_ref[...]).astype(BF16)

    o_ref[...] = jnp.dot(xn_ref[...], w_ref[...], preferred_element_type=F32)


def _in_proj(x2, gain, w, tm, tn):
    t, d = x2.shape
    n = w.shape[1]
    return pl.pallas_call(
        _in_proj_kernel,
        grid=(t // tm, n // tn),
        in_specs=[
            pl.BlockSpec((tm, d), lambda i, j: (i, 0)),
            pl.BlockSpec((1, d), lambda i, j: (0, 0)),
            pl.BlockSpec((d, tn), lambda i, j: (0, j)),
        ],
        out_specs=pl.BlockSpec((tm, tn), lambda i, j: (i, j)),
        out_shape=jax.ShapeDtypeStruct((t, n), F32),
        scratch_shapes=[pltpu.VMEM((tm, d), BF16)],
        compiler_params=_params("parallel", "arbitrary"),
        name="in_proj",
    )(x2, gain, w)


def _head_rms(x, gain):
    inv = lax.rsqrt(jnp.mean(x * x, axis=-1, keepdims=True) + RMS_EPS)
    return (x * inv) * gain


def _sb_attn_kernel(q_ref, k_ref, v_ref, gq_ref, gk_ref, o_ref, *, tq, scale):
    qi = pl.program_id(2)
    qb = (_head_rms(q_ref[...], gq_ref[...]) * scale).astype(BF16)
    row = lax.broadcasted_iota(jnp.int32, (tq, tq), 0)
    col = lax.broadcasted_iota(jnp.int32, (tq, tq), 1)
    causal = col < row
    later_keys = jnp.where(row > col, 1.0, 0.0).astype(BF16)

    def tile(j, carry, acc, diagonal):
        start = pl.multiple_of(j * tq, tq)
        kb = _head_rms(k_ref[pl.ds(start, tq), :], gk_ref[...]).astype(BF16)
        vb = v_ref[pl.ds(start, tq), :].astype(BF16)
        s = _nt_dot(qb, kb)
        soft = jnp.log1p(jnp.exp(-jnp.abs(s)))
        log_beta = jnp.minimum(s, 0.0) - soft
        log_keep = log_beta - s
        if diagonal:
            log_keep = jnp.where(causal, log_keep, 0.0)
        hi = log_keep.astype(BF16)
        lo = (log_keep - hi.astype(F32)).astype(BF16)
        later = (jnp.dot(hi, later_keys, preferred_element_type=F32)
                 + jnp.dot(lo, later_keys, preferred_element_type=F32))
        w = jnp.exp(log_beta + later + carry)
        if diagonal:
            w = jnp.where(causal, w, 0.0)
        acc = acc + jnp.dot(w.astype(BF16), vb, preferred_element_type=F32)
        carry = carry + jnp.sum(log_keep, axis=-1, keepdims=True)
        return carry, acc

    carry, acc = tile(qi, jnp.zeros((tq, 1), F32), jnp.zeros((tq, SB_HEAD_DIM), F32), True)

    def cond(state):
        j, carry, _ = state
        return jnp.logical_and(j >= 0, jnp.max(carry) > EXP_ZERO_BOUND)

    def body(state):
        j, carry, acc = state
        carry, acc = tile(j, carry, acc, False)
        return j - 1, carry, acc

    _, _, acc = lax.while_loop(cond, body, (qi - 1, carry, acc))
    o_ref[...] = acc


def _sb_attn(proj, gq, gk, heads, tq):
    b, s, _ = proj.shape
    kernel = functools.partial(_sb_attn_kernel, tq=tq, scale=SB_HEAD_DIM ** -0.5)
    return pl.pallas_call(
        kernel,
        grid=(b, heads, s // tq),
        in_specs=[
            pl.BlockSpec((None, tq, SB_HEAD_DIM), lambda bi, h, i: (bi, i, h)),
            pl.BlockSpec((None, s, SB_HEAD_DIM), lambda bi, h, i: (bi, 0, heads + h)),
            pl.BlockSpec((None, s, SB_HEAD_DIM), lambda bi, h, i: (bi, 0, 2 * heads + h)),
            pl.BlockSpec((1, SB_HEAD_DIM), lambda bi, h, i: (0, 0)),
            pl.BlockSpec((1, SB_HEAD_DIM), lambda bi, h, i: (0, 0)),
        ],
        out_specs=pl.BlockSpec((None, tq, SB_HEAD_DIM), lambda bi, h, i: (bi, i, h)),
        out_shape=jax.ShapeDtypeStruct((b, s, heads * SB_HEAD_DIM), F32),
        compiler_params=_params("parallel", "parallel", "arbitrary"),
        name="sb_attn",
    )(proj, proj, proj, gq, gk)


def _silu(x):
    return x * (1.0 / (1.0 + jnp.exp(-x)))


def _softplus(x):
    return jnp.maximum(x, 0.0) + jnp.log1p(jnp.exp(-jnp.abs(x)))


def _causal_conv(cur, ext_ref, w_ref, b_ref):
    l = cur.shape[0]
    ext_ref[pl.ds(SUBLANE, l), :] = cur
    out = b_ref[...] + w_ref[pl.ds(SSD_CONV - 1, 1), :] * cur
    for tap in range(SSD_CONV - 1):
        shift = SSD_CONV - 1 - tap
        out = out + w_ref[pl.ds(tap, 1), :] * ext_ref[pl.ds(SUBLANE - shift, l), :]
    ext_ref[pl.ds(0, SUBLANE), :] = cur[l - SUBLANE:, :]
    return out


def _ssd_kernel(xs_ref, bc_ref, dt_ref, z_ref, cwx_ref, cbx_ref, cwbc_ref, cbbc_ref,
                dtb_ref, alog_ref, dskip_ref, gain_ref, o_ref,
                extx_ref, extbc_ref, state_ref, *, heads):
    l = SSD_CHUNK
    n = SSD_STATE
    p = SSD_HEAD_DIM
    width = heads * p
    hg = heads // SSD_GROUPS
    gw = hg * p

    @pl.when(pl.program_id(1) == 0)
    def _():
        extx_ref[...] = jnp.zeros_like(extx_ref)
        extbc_ref[...] = jnp.zeros_like(extbc_ref)
        state_ref[...] = jnp.zeros_like(state_ref)

    xs = _silu(_causal_conv(xs_ref[...], extx_ref, cwx_ref, cbx_ref))
    bc = _silu(_causal_conv(bc_ref[...], extbc_ref, cwbc_ref, cbbc_ref))

    dt = _softplus(dt_ref[...] + dtb_ref[...])
    a_dt = dt * (-jnp.exp(alog_ref[...]))
    r = lax.broadcasted_iota(jnp.int32, (l, l), 0)
    c = lax.broadcasted_iota(jnp.int32, (l, l), 1)
    tril = r >= c
    a_cs = jnp.dot(jnp.where(tril, 1.0, 0.0), a_dt, precision=lax.Precision.HIGHEST,
                   preferred_element_type=F32)
    a_cs_t = a_cs.T

    er = lax.broadcasted_iota(jnp.int32, (LANE, width), 0)
    ec = lax.broadcasted_iota(jnp.int32, (LANE, width), 1)
    expand = jnp.where(ec // p == er, 1.0, 0.0)
    dt_e = jnp.dot(dt, expand, precision=lax.Precision.HIGHEST, preferred_element_type=F32)
    a_cs_e = jnp.dot(a_cs, expand, precision=lax.Precision.HIGHEST, preferred_element_type=F32)
    a_end_e = a_cs_e[l - 1:l, :]

    xf = xs * dt_e
    xd = (xf * jnp.exp(a_end_e - a_cs_e)).astype(BF16)
    xfb = xf.astype(BF16)
    decay_out = jnp.exp(a_cs_e)
    chunk_decay = jnp.exp(a_end_e)

    y_parts = []
    for g in range(SSD_GROUPS):
        bm = bc[:, g * n:(g + 1) * n]
        cm = bc[:, (SSD_GROUPS + g) * n:(SSD_GROUPS + g + 1) * n]
        bmb = bm.astype(BF16)
        cmb = cm.astype(BF16)
        cb = _nt_dot(cmb, bmb)
        cols = slice(g * gw, (g + 1) * gw)
        prev = state_ref[:, cols]
        y_off = jnp.dot(cmb, prev.astype(BF16), preferred_element_type=F32) * decay_out[:, cols]
        new_states = jnp.dot(bm.T.astype(BF16), xd[:, cols], preferred_element_type=F32)
        state_ref[:, cols] = prev * chunk_decay[:, cols] + new_states
        for hh in range(hg):
            h = g * hg + hh
            seg = a_cs[:, h:h + 1] - a_cs_t[h:h + 1, :]
            decay = jnp.where(tril, jnp.exp(jnp.where(tril, seg, 0.0)), 0.0)
            scores = (cb * decay).astype(BF16)
            y_diag = jnp.dot(scores, xfb[:, h * p:(h + 1) * p], preferred_element_type=F32)
            y_parts.append(y_diag + y_off[:, hh * p:(hh + 1) * p])
    y = jnp.concatenate(y_parts, axis=-1) + dskip_ref[...] * xs

    y = y * _silu(z_ref[...])
    outs = []
    for g in range(SSD_GROUPS):
        yg = y[:, g * gw:(g + 1) * gw]
        inv = lax.rsqrt(jnp.mean(yg * yg, axis=-1, keepdims=True) + RMS_EPS)
        outs.append(yg * inv)
    o_ref[...] = jnp.concatenate(outs, axis=-1) * gain_ref[...]


def _ssd(proj, cwx, cbx, cwbc, cbbc, dtb, alog, dskip_e, gain, heads, sb_width):
    b, s, _ = proj.shape
    l = SSD_CHUNK
    width = heads * SSD_HEAD_DIM
    bcw = 2 * SSD_GROUPS * SSD_STATE
    z_off = 3 * sb_width
    xs_off = z_off + width
    bc_off = xs_off + width
    dt_off = bc_off + bcw
    assert z_off % width == 0 and xs_off % width == 0 and bc_off % bcw == 0 and dt_off % LANE == 0
    const = lambda shape: pl.BlockSpec(shape, lambda bi, ci: (0, 0))
    kernel = functools.partial(_ssd_kernel, heads=heads)
    return pl.pallas_call(
        kernel,
        grid=(b, s // l),
        in_specs=[
            pl.BlockSpec((None, l, width), lambda bi, ci: (bi, ci, xs_off // width)),
            pl.BlockSpec((None, l, bcw), lambda bi, ci: (bi, ci, bc_off // bcw)),
            pl.BlockSpec((None, l, LANE), lambda bi, ci: (bi, ci, dt_off // LANE)),
            pl.BlockSpec((None, l, width), lambda bi, ci: (bi, ci, z_off // width)),
            const((SSD_CONV, width)), const((1, width)),
            const((SSD_CONV, bcw)), const((1, bcw)),
            const((1, LANE)), const((1, LANE)), const((1, width)), const((1, width)),
        ],
        out_specs=pl.BlockSpec((None, l, width), lambda bi, ci: (bi, ci, 0)),
        out_shape=jax.ShapeDtypeStruct((b, s, width), F32),
        scratch_shapes=[
            pltpu.VMEM((SUBLANE + l, width), F32),
            pltpu.VMEM((SUBLANE + l, bcw), F32),
            pltpu.VMEM((SSD_STATE, width), F32),
        ],
        compiler_params=_params("parallel", "arbitrary"),
        name="ssd",
    )(proj, proj, proj, proj, cwx, cbx, cwbc, cbbc, dtb, alog, dskip_e, gain)


def _out_proj_kernel(attn_ref, ssd_ref, x_ref, ga_ref, wa_ref, ws_ref, gf_ref, h_ref, hn_ref):
    a = attn_ref[...]
    inv = lax.rsqrt(jnp.mean(a * a, axis=-1, keepdims=True) + RMS_EPS)
    an = ((a * inv) * ga_ref[...]).astype(BF16)
    h = (x_ref[...]
         + jnp.dot(an, wa_ref[...], preferred_element_type=F32)
         + jnp.dot(ssd_ref[...].astype(BF16), ws_ref[...], preferred_element_type=F32))
    h_ref[...] = h
    inv_h = lax.rsqrt(jnp.mean(h * h, axis=-1, keepdims=True) + RMS_EPS)
    hn_ref[...] = ((h * inv_h) * gf_ref[...]).astype(BF16)


def _out_proj(attn2, ssd2, x2, ga, wa, ws, gf, tm):
    t, d = x2.shape
    wa_rows, ws_rows = wa.shape[0], ws.shape[0]
    return pl.pallas_call(
        _out_proj_kernel,
        grid=(t // tm,),
        in_specs=[
            pl.BlockSpec((tm, wa_rows), lambda i: (i, 0)),
            pl.BlockSpec((tm, ws_rows), lambda i: (i, 0)),
            pl.BlockSpec((tm, d), lambda i: (i, 0)),
            pl.BlockSpec((1, wa_rows), lambda i: (0, 0)),
            pl.BlockSpec((wa_rows, d), lambda i: (0, 0)),
            pl.BlockSpec((ws_rows, d), lambda i: (0, 0)),
            pl.BlockSpec((1, d), lambda i: (0, 0)),
        ],
        out_specs=[pl.BlockSpec((tm, d), lambda i: (i, 0)),
                   pl.BlockSpec((tm, d), lambda i: (i, 0))],
        out_shape=[jax.ShapeDtypeStruct((t, d), F32), jax.ShapeDtypeStruct((t, d), BF16)],
        compiler_params=_params("parallel"),
        name="out_proj",
    )(attn2, ssd2, x2, ga, wa, ws, gf)


def _top_k_rows(scores, k):
    rows = scores.shape[0]
    iota = lax.broadcasted_iota(jnp.int32, scores.shape, 0).astype(F32)
    work = scores
    rank = jnp.full(scores.shape, float(k), F32)
    tops = []
    for r in range(k):
        m = jnp.max(work, axis=0, keepdims=True)
        idx = jnp.min(jnp.where(work == m, iota, float(rows)), axis=0, keepdims=True)
        hit = iota == idx
        rank = jnp.where(hit, float(r), rank)
        work = jnp.where(hit, NEG_INF, work)
        tops.append(m)
    return tops, rank


def _peer_route_kernel(hn_ref, wq_ref, keys_ref, w0_ref, n0_ref, w1_ref, r1_ref, *, half):
    h = pl.program_id(1)
    k = PEER_TOPK
    q = jnp.dot(hn_ref[...], wq_ref[h], preferred_element_type=F32)
    s0 = _nt_dot(keys_ref[h, 0], q[:, :half], precision=lax.Precision.HIGHEST)
    s1 = _nt_dot(keys_ref[h, 1], q[:, half:], precision=lax.Precision.HIGHEST)
    tops0, rank0 = _top_k_rows(s0, k)
    tops1, rank1 = _top_k_rows(s1, k)

    t1 = jnp.concatenate(tops1, axis=0)
    cand = jnp.concatenate([tops0[a] + t1 for a in range(k)], axis=0)
    bests, crank = _top_k_rows(cand, k)
    sel = jnp.where(crank < float(k), 1.0, 0.0)
    z = jnp.zeros_like(bests[0])
    for r in range(k):
        z = z + jnp.exp(bests[r] - bests[0])
    n0 = jnp.zeros_like(s0)
    for a in range(k):
        n_a = jnp.sum(sel[a * k:(a + 1) * k, :], axis=0, keepdims=True)
        n0 = jnp.where(rank0 == float(a), n_a, n0)

    w0_ref[...] = jnp.exp(s0 - tops0[0]) / z
    n0_ref[...] = n0
    w1_ref[...] = jnp.exp(s1 - tops1[0])
    r1_ref[...] = rank1


def _peer_route(hn, wq_h, keys, tm):
    t, d = hn.shape
    heads, _, qw = wq_h.shape
    n_keys, half = keys.shape[2], keys.shape[3]
    out = jax.ShapeDtypeStruct((heads, n_keys, t), F32)
    ospec = pl.BlockSpec((None, n_keys, tm), lambda i, h: (h, 0, i))
    kernel = functools.partial(_peer_route_kernel, half=half)
    return pl.pallas_call(
        kernel,
        grid=(t // tm, heads),
        in_specs=[
            pl.BlockSpec((tm, d), lambda i, h: (i, 0)),
            pl.BlockSpec((heads, d, qw), lambda i, h: (0, 0, 0)),
            pl.BlockSpec((heads, 2, n_keys, half), lambda i, h: (0, 0, 0, 0)),
        ],
        out_specs=[ospec, ospec, ospec, ospec],
        out_shape=[out, out, out, out],
        compiler_params=_params("parallel", "arbitrary"),
        name="peer_route",
    )(hn, wq_h, keys)


def _gelu(x):
    return 0.5 * x * (1.0 + lax.erf(x * (1.0 / math.sqrt(2.0))))


def _peer_experts_kernel(hn_ref, down_ref, upt_ref, w0_ref, n0_ref, w1_ref, r1_ref, h_ref,
                         y_ref, act_ref, p_ref, acc_ref, *, heads, n_keys, rows_per_step):
    e = pl.program_id(1)
    tb = hn_ref.shape[0]

    @pl.when(e == 0)
    def _():
        acc_ref[...] = jnp.zeros_like(acc_ref)

    act_ref[...] = _gelu(_nt_dot(down_ref[...], hn_ref[...]))

    for r in range(rows_per_step):
        for tc in range(tb // LANE):
            lanes = pl.ds(tc * LANE, LANE)
            gate = jnp.zeros((n_keys, LANE), F32)
            for h in range(heads):
                n_row = n0_ref[h, pl.ds(r, 1), lanes]
                w_row = w0_ref[h, pl.ds(r, 1), lanes]
                gate = gate + jnp.where(r1_ref[h, :, lanes] < n_row, w_row * w1_ref[h, :, lanes], 0.0)
            rows = pl.ds(r * n_keys, n_keys)
            p_ref[rows, lanes] = (gate * act_ref[rows, lanes]).astype(BF16)

    acc_ref[...] += jnp.dot(upt_ref[...], p_ref[...], preferred_element_type=F32)

    @pl.when(e == pl.num_programs(1) - 1)
    def _():
        y_ref[...] = h_ref[...] + acc_ref[...].T


def _peer_experts(hn, down, upt, w0, n0, w1, r1, h2, tb, rows_per_step):
    t, d = hn.shape
    heads, n_keys, _ = w0.shape
    ec = rows_per_step * n_keys
    n_exp = down.shape[0]
    fspec = pl.BlockSpec((heads, n_keys, tb), lambda i, e: (0, 0, i))
    rspec = pl.BlockSpec((heads, rows_per_step, tb), lambda i, e: (0, e, i))
    kernel = functools.partial(_peer_experts_kernel, heads=heads, n_keys=n_keys,
                               rows_per_step=rows_per_step)
    return pl.pallas_call(
        kernel,
        grid=(t // tb, n_exp // ec),
        in_specs=[
            pl.BlockSpec((tb, d), lambda i, e: (i, 0)),
            pl.BlockSpec((ec, d), lambda i, e: (e, 0)),
            pl.BlockSpec((d, ec), lambda i, e: (0, e)),
            rspec, rspec, fspec, fspec,
            pl.BlockSpec((tb, d), lambda i, e: (i, 0), pipeline_mode=pl.Buffered(1)),
        ],
        out_specs=pl.BlockSpec((tb, d), lambda i, e: (i, 0)),
        out_shape=jax.ShapeDtypeStruct((t, d), F32),
        scratch_shapes=[
            pltpu.VMEM((ec, tb), F32),
            pltpu.VMEM((ec, tb), BF16),
            pltpu.VMEM((d, tb), F32),
        ],
        compiler_params=_params("parallel", "arbitrary"),
        name="peer_experts",
    )(hn, down, upt, w0, n0, w1, r1, h2)


def _largest_tile(total, limit, unit):
    best = unit
    for cand in range(unit, min(total, limit) + 1, unit):
        if total % cand == 0:
            best = cand
    return best


def _pad_cols(a, cols):
    return jnp.pad(a, ((0, 0), (0, cols - a.shape[1])))


def kernel(x, attn_norm, w_in, sb_q_norm, sb_k_norm, conv_w, conv_b, dt_bias, a_log, d_skip,
           sb_out_norm, ssd_out_norm, w_out, ffn_norm, peer_query, peer_sub_keys, peer_down, peer_up):
    b, s, d = x.shape
    t = b * s
    depth = w_in.shape[0]
    sb_width = sb_out_norm.shape[-1]
    sb_heads = sb_width // SB_HEAD_DIM
    ssd_width = ssd_out_norm.shape[-1]
    ssd_heads = dt_bias.shape[-1]
    assert ssd_heads * SSD_HEAD_DIM == ssd_width and ssd_heads <= LANE
    peer_heads, _, n_keys, half = peer_sub_keys.shape[1:]
    n_in = w_in.shape[-1]
    n_in_pad = -(-n_in // LANE) * LANE

    tm_in = _largest_tile(t, 512, SUBLANE)
    tn_in = _largest_tile(n_in_pad, 1152, LANE)
    tq = _largest_tile(s, 256, LANE)
    tm_out = _largest_tile(t, 256, SUBLANE)
    tm_route = _largest_tile(t, 256, LANE)
    tb_peer = _largest_tile(t, 512, LANE)
    rows_per_step = _largest_tile(n_keys, SUBLANE, SUBLANE)

    h2 = x.reshape(t, d)
    for layer in range(depth):
        w_in_l = _pad_cols(w_in[layer], n_in_pad).astype(BF16)
        proj = _in_proj(h2, attn_norm[layer][None, :], w_in_l, tm_in, tn_in).reshape(b, s, n_in_pad)

        attn = _sb_attn(proj, sb_q_norm[layer][None, :], sb_k_norm[layer][None, :], sb_heads, tq)

        cw = conv_w[layer][:, 0, :]
        cb = conv_b[layer][None, :]
        ssd = _ssd(proj, cw[:, :ssd_width], cb[:, :ssd_width], cw[:, ssd_width:], cb[:, ssd_width:],
                   _pad_cols(dt_bias[layer][None, :], LANE), _pad_cols(a_log[layer][None, :], LANE),
                   jnp.repeat(d_skip[layer], SSD_HEAD_DIM)[None, :], ssd_out_norm[layer][None, :],
                   ssd_heads, sb_width)

        w_out_l = w_out[layer].astype(BF16)
        h2, hn = _out_proj(attn.reshape(t, sb_width), ssd.reshape(t, ssd_width), h2,
                           sb_out_norm[layer][None, :], w_out_l[:sb_width], w_out_l[sb_width:],
                           ffn_norm[layer][None, :], tm_out)

        wq_h = peer_query[layer].astype(BF16).reshape(d, peer_heads, 2 * half).transpose(1, 0, 2)
        w0, n0, w1, r1 = _peer_route(hn, wq_h, peer_sub_keys[layer], tm_route)
        h2 = _peer_experts(hn, peer_down[layer].astype(BF16), peer_up[layer].astype(BF16).T,
                           w0, n0, w1, r1, h2, tb_peer, rows_per_step)
    return h2.reshape(b, s, d)
```

```python
import functools
import math

import jax
import jax.numpy as jnp
from jax import lax
from jax.experimental import pallas as pl
from jax.experimental.pallas import tpu as pltpu

F32 = jnp.float32
BF16 = jnp.bfloat16

RMS_EPS = 1e-6
SB_HEAD_DIM = 128
SSD_HEAD_DIM = 64
SSD_GROUPS = 2
SSD_STATE = 128
SSD_CONV = 4
SSD_CHUNK = 128
PEER_TOPK = 16
LANE = 128
SUBLANE = 8
BF16_ROWS = 16
MXU_ACC_ROWS = 512
VMEM_LIMIT = 56 * 1024 * 1024

EXP_ZERO_BOUND = -105.0

NEG_INF = float("-inf")


def _nt_dot(a, b, precision=None):
    return lax.dot_general(a, b, (((1,), (1,)), ((), ())), precision=precision,
                           preferred_element_type=F32)


def _params(*sem):
    return pltpu.CompilerParams(dimension_semantics=sem, vmem_limit_bytes=VMEM_LIMIT)


def _in_proj_kernel(x_ref, g_ref, w_ref, o_ref, xn_ref):
    @pl.when(pl.program_id(1) == 0)
    def _():
        x = x_ref[...]
        inv = lax.rsqrt(jnp.mean(x * x, axis=-1, keepdims=True) + RMS_EPS)
        xn_ref[...] = ((x * inv) * g_ref[...]).astype(BF16)

    o_ref[...] = jnp.dot(xn_ref[...], w_ref[...], preferred_element_type=F32)


def _in_proj(x2, gain, w, tm, tn):
    t, d = x2.shape
    n = w.shape[1]
    return pl.pallas_call(
        _in_proj_kernel,
        grid=(t // tm, n // tn),
        in_specs=[
            pl.BlockSpec((tm, d), lambda i, j: (i, 0)),
            pl.BlockSpec((1, d), lambda i, j: (0, 0)),
            pl.BlockSpec((d, tn), lambda i, j: (0, j)),
        ],
        out_specs=pl.BlockSpec((tm, tn), lambda i, j: (i, j)),
        out_shape=jax.ShapeDtypeStruct((t, n), F32),
        scratch_shapes=[pltpu.VMEM((tm, d), BF16)],
        compiler_params=_params("parallel", "arbitrary"),
        name="in_proj",
    )(x2, gain, w)


def _head_rms(x, gain):
    inv = lax.rsqrt(jnp.mean(x * x, axis=-1, keepdims=True) + RMS_EPS)
    return (x * inv) * gain


def _sb_attn_kernel(q_ref, k_ref, v_ref, gq_ref, gk_ref, o_ref, *, tq, scale):
    qi = pl.program_id(2)
    qb = (_head_rms(q_ref[...], gq_ref[...]) * scale).astype(BF16)
    row = lax.broadcasted_iota(jnp.int32, (tq, tq), 0)
    col = lax.broadcasted_iota(jnp.int32, (tq, tq), 1)
    causal = col < row
    later_keys = jnp.where(row > col, 1.0, 0.0).astype(BF16)

    def tile(j, carry, acc, diagonal):
        start = pl.multiple_of(j * tq, tq)
        kb = _head_rms(k_ref[pl.ds(start, tq), :], gk_ref[...]).astype(BF16)
        vb = v_ref[pl.ds(start, tq), :].astype(BF16)
        s = _nt_dot(qb, kb)
        soft = jnp.log1p(jnp.exp(-jnp.abs(s)))
        log_beta = jnp.minimum(s, 0.0) - soft
        log_keep = log_beta - s
        if diagonal:
            log_keep = jnp.where(causal, log_keep, 0.0)
        hi = log_keep.astype(BF16)
        lo = (log_keep - hi.astype(F32)).astype(BF16)
        later = (jnp.dot(hi, later_keys, preferred_element_type=F32)
                 + jnp.dot(lo, later_keys, preferred_element_type=F32))
        w = jnp.exp(log_beta + later + carry)
        if diagonal:
            w = jnp.where(causal, w, 0.0)
        acc = acc + jnp.dot(w.astype(BF16), vb, preferred_element_type=F32)
        carry = carry + jnp.sum(log_keep, axis=-1, keepdims=True)
        return carry, acc

    carry, acc = tile(qi, jnp.zeros((tq, 1), F32), jnp.zeros((tq, SB_HEAD_DIM), F32), True)

    def cond(state):
        j, carry, _ = state
        return jnp.logical_and(j >= 0, jnp.max(carry) > EXP_ZERO_BOUND)

    def body(state):
        j, carry, acc = state
        carry, acc = tile(j, carry, acc, False)
        return j - 1, carry, acc

    _, _, acc = lax.while_loop(cond, body, (qi - 1, carry, acc))
    o_ref[...] = acc


def _sb_attn(proj, gq, gk, heads, tq):
    b, s, _ = proj.shape
    kernel = functools.partial(_sb_attn_kernel, tq=tq, scale=SB_HEAD_DIM ** -0.5)
    return pl.pallas_call(
        kernel,
        grid=(b, heads, s // tq),
        in_specs=[
            pl.BlockSpec((None, tq, SB_HEAD_DIM), lambda bi, h, i: (bi, i, h)),
            pl.BlockSpec((None, s, SB_HEAD_DIM), lambda bi, h, i: (bi, 0, heads + h)),
            pl.BlockSpec((None, s, SB_HEAD_DIM), lambda bi, h, i: (bi, 0, 2 * heads + h)),
            pl.BlockSpec((1, SB_HEAD_DIM), lambda bi, h, i: (0, 0)),
            pl.BlockSpec((1, SB_HEAD_DIM), lambda bi, h, i: (0, 0)),
        ],
        out_specs=pl.BlockSpec((None, tq, SB_HEAD_DIM), lambda bi, h, i: (bi, i, h)),
        out_shape=jax.ShapeDtypeStruct((b, s, heads * SB_HEAD_DIM), F32),
        compiler_params=_params("parallel", "parallel", "arbitrary"),
        name="sb_attn",
    )(proj, proj, proj, gq, gk)


def _silu(x):
    return x * (1.0 / (1.0 + jnp.exp(-x)))


def _softplus(x):
    return jnp.maximum(x, 0.0) + jnp.log1p(jnp.exp(-jnp.abs(x)))


def _causal_conv(cur, ext_ref, w_ref, b_ref):
    l = cur.shape[0]
    ext_ref[pl.ds(SUBLANE, l), :] = cur
    out = b_ref[...] + w_ref[pl.ds(SSD_CONV - 1, 1), :] * cur
    for tap in range(SSD_CONV - 1):
        shift = SSD_CONV - 1 - tap
        out = out + w_ref[pl.ds(tap, 1), :] * ext_ref[pl.ds(SUBLANE - shift, l), :]
    ext_ref[pl.ds(0, SUBLANE), :] = cur[l - SUBLANE:, :]
    return out


def _ssd_kernel(xs_ref, bc_ref, dt_ref, z_ref, cwx_ref, cbx_ref, cwbc_ref, cbbc_ref,
                dtb_ref, alog_ref, dskip_ref, gain_ref, o_ref,
                extx_ref, extbc_ref, state_ref, *, heads):
    l = SSD_CHUNK
    n = SSD_STATE
    p = SSD_HEAD_DIM
    width = heads * p
    hg = heads // SSD_GROUPS
    gw = hg * p

    @pl.when(pl.program_id(1) == 0)
    def _():
        extx_ref[...] = jnp.zeros_like(extx_ref)
        extbc_ref[...] = jnp.zeros_like(extbc_ref)
        state_ref[...] = jnp.zeros_like(state_ref)

    xs = _silu(_causal_conv(xs_ref[...], extx_ref, cwx_ref, cbx_ref))
    bc = _silu(_causal_conv(bc_ref[...], extbc_ref, cwbc_ref, cbbc_ref))

    dt = _softplus(dt_ref[...] + dtb_ref[...])
    a_dt = dt * (-jnp.exp(alog_ref[...]))
    r = lax.broadcasted_iota(jnp.int32, (l, l), 0)
    c = lax.broadcasted_iota(jnp.int32, (l, l), 1)
    tril = r >= c
    a_cs = jnp.dot(jnp.where(tril, 1.0, 0.0), a_dt, precision=lax.Precision.HIGHEST,
                   preferred_element_type=F32)
    a_cs_t = a_cs.T

    er = lax.broadcasted_iota(jnp.int32, (LANE, width), 0)
    ec = lax.broadcasted_iota(jnp.int32, (LANE, width), 1)
    expand = jnp.where(ec // p == er, 1.0, 0.0)
    dt_e = jnp.dot(dt, expand, precision=lax.Precision.HIGHEST, preferred_element_type=F32)
    a_cs_e = jnp.dot(a_cs, expand, precision=lax.Precision.HIGHEST, preferred_element_type=F32)
    a_end_e = a_cs_e[l - 1:l, :]

    xf = xs * dt_e
    xd = (xf * jnp.exp(a_end_e - a_cs_e)).astype(BF16)
    xfb = xf.astype(BF16)
    decay_out = jnp.exp(a_cs_e)
    chunk_decay = jnp.exp(a_end_e)

    y_parts = []
    for g in range(SSD_GROUPS):
        bm = bc[:, g * n:(g + 1) * n]
        cm = bc[:, (SSD_GROUPS + g) * n:(SSD_GROUPS + g + 1) * n]
        bmb = bm.astype(BF16)
        cmb = cm.astype(BF16)
        cb = _nt_dot(cmb, bmb)
        cols = slice(g * gw, (g + 1) * gw)
        prev = state_ref[:, cols]
        y_off = jnp.dot(cmb, prev.astype(BF16), preferred_element_type=F32) * decay_out[:, cols]
        new_states = jnp.dot(bm.T.astype(BF16), xd[:, cols], preferred_element_type=F32)
        state_ref[:, cols] = prev * chunk_decay[:, cols] + new_states
        for hh in range(hg):
            h = g * hg + hh
            seg = a_cs[:, h:h + 1] - a_cs_t[h:h + 1, :]
            decay = jnp.where(tril, jnp.exp(jnp.where(tril, seg, 0.0)), 0.0)
            scores = (cb * decay).astype(BF16)
            y_diag = jnp.dot(scores, xfb[:, h * p:(h + 1) * p], preferred_element_type=F32)
            y_parts.append(y_diag + y_off[:, hh * p:(hh + 1) * p])
    y = jnp.concatenate(y_parts, axis=-1) + dskip_ref[...] * xs

    y = y * _silu(z_ref[...])
    outs = []
    for g in range(SSD_GROUPS):
        yg = y[:, g * gw:(g + 1) * gw]
        inv = lax.rsqrt(jnp.mean(yg * yg, axis=-1, keepdims=True) + RMS_EPS)
        outs.append(yg * inv)
    o_ref[...] = jnp.concatenate(outs, axis=-1) * gain_ref[...]


def _ssd(proj, cwx, cbx, cwbc, cbbc, dtb, alog, dskip_e, gain, heads, sb_width):
    b, s, _ = proj.shape
    l = SSD_CHUNK
    width = heads * SSD_HEAD_DIM
    bcw = 2 * SSD_GROUPS * SSD_STATE
    z_off = 3 * sb_width
    xs_off = z_off + width
    bc_off = xs_off + width
    dt_off = bc_off + bcw
    assert z_off % width == 0 and xs_off % width == 0 and bc_off % bcw == 0 and dt_off % LANE == 0
    const = lambda shape: pl.BlockSpec(shape, lambda bi, ci: (0, 0))
    kernel = functools.partial(_ssd_kernel, heads=heads)
    return pl.pallas_call(
        kernel,
        grid=(b, s // l),
        in_specs=[
            pl.BlockSpec((None, l, width), lambda bi, ci: (bi, ci, xs_off // width)),
            pl.BlockSpec((None, l, bcw), lambda bi, ci: (bi, ci, bc_off // bcw)),
            pl.BlockSpec((None, l, LANE), lambda bi, ci: (bi, ci, dt_off // LANE)),
            pl.BlockSpec((None, l, width), lambda bi, ci: (bi, ci, z_off // width)),
            const((SSD_CONV, width)), const((1, width)),
            const((SSD_CONV, bcw)), const((1, bcw)),
            const((1, LANE)), const((1, LANE)), const((1, width)), const((1, width)),
        ],
        out_specs=pl.BlockSpec((None, l, width), lambda bi, ci: (bi, ci, 0)),
        out_shape=jax.ShapeDtypeStruct((b, s, width), F32),
        scratch_shapes=[
            pltpu.VMEM((SUBLANE + l, width), F32),
            pltpu.VMEM((SUBLANE + l, bcw), F32),
            pltpu.VMEM((SSD_STATE, width), F32),
        ],
        compiler_params=_params("parallel", "arbitrary"),
        name="ssd",
    )(proj, proj, proj, proj, cwx, cbx, cwbc, cbbc, dtb, alog, dskip_e, gain)


def _out_proj_kernel(attn_ref, ssd_ref, x_ref, ga_ref, wa_ref, ws_ref, gf_ref, h_ref, hn_ref):
    a = attn_ref[...]
    inv = lax.rsqrt(jnp.mean(a * a, axis=-1, keepdims=True) + RMS_EPS)
    an = ((a * inv) * ga_ref[...]).astype(BF16)
    h = (x_ref[...]
         + jnp.dot(an, wa_ref[...], preferred_element_type=F32)
         + jnp.dot(ssd_ref[...].astype(BF16), ws_ref[...], preferred_element_type=F32))
    h_ref[...] = h
    inv_h = lax.rsqrt(jnp.mean(h * h, axis=-1, keepdims=True) + RMS_EPS)
    hn_ref[...] = ((h * inv_h) * gf_ref[...]).astype(BF16)


def _out_proj(attn2, ssd2, x2, ga, wa, ws, gf, tm):
    t, d = x2.shape
    wa_rows, ws_rows = wa.shape[0], ws.shape[0]
    return pl.pallas_call(
        _out_proj_kernel,
        grid=(t // tm,),
        in_specs=[
            pl.BlockSpec((tm, wa_rows), lambda i: (i, 0)),
            pl.BlockSpec((tm, ws_rows), lambda i: (i, 0)),
            pl.BlockSpec((tm, d), lambda i: (i, 0)),
            pl.BlockSpec((1, wa_rows), lambda i: (0, 0)),
            pl.BlockSpec((wa_rows, d), lambda i: (0, 0)),
            pl.BlockSpec((ws_rows, d), lambda i: (0, 0)),
            pl.BlockSpec((1, d), lambda i: (0, 0)),
        ],
        out_specs=[pl.BlockSpec((tm, d), lambda i: (i, 0)),
                   pl.BlockSpec((tm, d), lambda i: (i, 0))],
        out_shape=[jax.ShapeDtypeStruct((t, d), F32), jax.ShapeDtypeStruct((t, d), BF16)],
        compiler_params=_params("parallel"),
        name="out_proj",
    )(attn2, ssd2, x2, ga, wa, ws, gf)


def _top_k_rows(scores, k, pos=None):
    work = scores
    rank = jnp.full(scores.shape, float(k), F32)
    tops = []
    for r in range(k):
        m = jnp.max(work, axis=0, keepdims=True)
        hit = work == m
        if pos is not None:
            first = jnp.min(jnp.where(hit, pos, float("inf")), axis=0, keepdims=True)
            hit = pos == first
        rank = jnp.where(hit, float(r), rank)
        work = jnp.where(hit, NEG_INF, work)
        tops.append(m)
    count = jnp.sum(jnp.where(rank < float(k), 1.0, 0.0), axis=0, keepdims=True)
    return tops, rank, count


def _row_iota(rows, cols):
    return lax.broadcasted_iota(jnp.int32, (rows, cols), 0).astype(F32)


def _peer_gate_factors(s0, s1, k, tie_safe):
    assert k == 2 * SUBLANE
    keys, tm = s0.shape
    pos = _row_iota(keys, tm) if tie_safe else None
    tops0, rank0, count0 = _top_k_rows(s0, k, pos)
    tops1, rank1, count1 = _top_k_rows(s1, k, pos)
    t0 = jnp.concatenate(tops0, axis=0)
    t1 = jnp.concatenate(tops1, axis=0)
    i8 = _row_iota(SUBLANE, tm)
    blocks = [tops0[0] + t1, tops0[1] + t1[:SUBLANE]]
    cpos = [_row_iota(k, tm), float(k) + i8]
    for a in range(2, SUBLANE):
        blocks.append(jnp.where(i8 < float(k // (a + 1)), tops0[a] + t1[:SUBLANE], NEG_INF))
        cpos.append(float(a * k) + i8)
    blocks.append(t0[SUBLANE:] + tops1[0])
    cpos.append((i8 + float(SUBLANE)) * float(k))
    cand = jnp.concatenate(blocks, axis=0)
    bests, crank, count2 = _top_k_rows(cand, k, jnp.concatenate(cpos, axis=0) if tie_safe else None)

    sel = jnp.where(crank < float(k), 1.0, 0.0)
    z = jnp.zeros_like(bests[0])
    for r in range(k):
        z = z + jnp.exp(bests[r] - bests[0])
    n0 = jnp.zeros_like(s0)
    row = 0
    for a in range(k):
        rows_a = k if a == 0 else (SUBLANE if a < SUBLANE else 1)
        n_a = jnp.sum(sel[row:row + rows_a, :], axis=0, keepdims=True)
        n0 = jnp.where(rank0 == float(a), n_a, n0)
        row += rows_a
    w0 = (0.5 * jnp.exp(s0 - tops0[0])) / z
    w1 = jnp.exp(s1 - tops1[0])
    most = jnp.max(jnp.maximum(jnp.maximum(count0, count1), count2))
    return w0, n0, w1, rank1, most


def _peer_route_kernel(hn_ref, wq_ref, keys_ref, w0_ref, n0_ref, w1_ref, r1_ref, *, half):
    h = pl.program_id(1)
    k = PEER_TOPK
    q = jnp.dot(hn_ref[...], wq_ref[h], preferred_element_type=F32)
    s0 = _nt_dot(keys_ref[h, 0], q[:, :half], precision=lax.Precision.HIGHEST)
    s1 = _nt_dot(keys_ref[h, 1], q[:, half:], precision=lax.Precision.HIGHEST)

    def emit(tie_safe):
        w0, n0, w1, r1, most = _peer_gate_factors(s0, s1, k, tie_safe)
        w0_ref[...] = w0
        n0_ref[...] = n0
        w1_ref[...] = w1.astype(BF16)
        r1_ref[...] = r1.astype(BF16)
        return most

    most = emit(tie_safe=False)

    @pl.when(most > float(k))
    def _():
        emit(tie_safe=True)


def _peer_route(hn, wq_h, keys, tm):
    t, d = hn.shape
    heads, _, qw = wq_h.shape
    n_keys, half = keys.shape[2], keys.shape[3]
    out32 = jax.ShapeDtypeStruct((heads, n_keys, t), F32)
    out16 = jax.ShapeDtypeStruct((heads, n_keys, t), BF16)
    ospec = pl.BlockSpec((None, n_keys, tm), lambda i, h: (h, 0, i))
    kernel = functools.partial(_peer_route_kernel, half=half)
    return pl.pallas_call(
        kernel,
        grid=(t // tm, heads),
        in_specs=[
            pl.BlockSpec((tm, d), lambda i, h: (i, 0)),
            pl.BlockSpec((heads, d, qw), lambda i, h: (0, 0, 0)),
            pl.BlockSpec((heads, 2, n_keys, half), lambda i, h: (0, 0, 0, 0)),
        ],
        out_specs=[ospec, ospec, ospec, ospec],
        out_shape=[out32, out32, out16, out16],
        compiler_params=_params("parallel", "arbitrary"),
        name="peer_route",
    )(hn, wq_h, keys)


def _peer_experts_kernel(hn_ref, down_ref, upt_ref, w0_ref, n0_ref, w1_ref, r1_ref, h_ref,
                         y_ref, gate_ref, p_ref, acc_ref, *, heads, n_keys, rows_per_step):
    e = pl.program_id(1)
    n_chunks = pl.num_programs(1) - 1
    tb = hn_ref.shape[0]

    @pl.when(e == 0)
    def _():
        acc_ref[...] = jnp.zeros_like(acc_ref)
        p_ref[...] = jnp.zeros_like(p_ref)

    n_slabs = acc_ref.shape[0] // MXU_ACC_ROWS

    def up_project(s):
        rows = pl.ds(s * MXU_ACC_ROWS, MXU_ACC_ROWS)
        acc_ref[rows, :] += jnp.dot(upt_ref[rows, :], p_ref[...], preferred_element_type=F32)

    def build_gates(r):
        for tc in range(tb // LANE):
            lanes = pl.ds(tc * LANE, LANE)
            gate = jnp.zeros((n_keys // BF16_ROWS, BF16_ROWS, LANE), BF16)
            for h in range(heads):
                n_row = jnp.broadcast_to(n0_ref[h, pl.ds(r, 1), lanes], (BF16_ROWS, LANE))
                w_row = jnp.broadcast_to(w0_ref[h, pl.ds(r, 1), lanes], (BF16_ROWS, LANE))
                n_row = n_row.astype(BF16)[None]
                w_row = w_row.astype(BF16)[None]
                live = jnp.maximum(n_row - r1_ref[h, :, :, lanes], 0.0)
                gate = gate + w_row * jnp.minimum(live, w1_ref[h, :, :, lanes])
            gate_ref[pl.ds(r * n_keys, n_keys), lanes] = gate.reshape(n_keys, LANE)

    @pl.when(e < n_chunks)
    def _():
        for s in range(n_slabs):
            for r in range(s * rows_per_step // n_slabs, (s + 1) * rows_per_step // n_slabs):
                build_gates(r)
            up_project(s)

    @pl.when(e != n_chunks)
    def _():
        a = _nt_dot(down_ref[...], hn_ref[...])
        act = a * (1.0 + lax.erf(a * (1.0 / math.sqrt(2.0))))
        p_ref[...] = gate_ref[...] * act.astype(BF16)

    @pl.when(e == n_chunks)
    def _():
        for s in range(n_slabs):
            up_project(s)
        y_ref[...] = h_ref[...] + acc_ref[...].T


def _peer_experts(hn, down, upt, w0, n0, w1, r1, h2, tb, rows_per_step):
    t, d = hn.shape
    heads, n_keys, _ = w0.shape
    ec = rows_per_step * n_keys
    n_chunks = down.shape[0] // ec
    last = n_chunks - 1
    w1 = w1.reshape(heads, n_keys // BF16_ROWS, BF16_ROWS, t)
    r1 = r1.reshape(heads, n_keys // BF16_ROWS, BF16_ROWS, t)
    fspec = pl.BlockSpec((heads, n_keys // BF16_ROWS, BF16_ROWS, tb), lambda i, e: (0, 0, 0, i))
    rspec = pl.BlockSpec((heads, rows_per_step, tb), lambda i, e: (0, jnp.minimum(e, last), i))
    kernel = functools.partial(_peer_experts_kernel, heads=heads, n_keys=n_keys,
                               rows_per_step=rows_per_step)
    return pl.pallas_call(
        kernel,
        grid=(t // tb, n_chunks + 1),
        in_specs=[
            pl.BlockSpec((tb, d), lambda i, e: (i, 0)),
            pl.BlockSpec((ec, d), lambda i, e: (jnp.minimum(e, last), 0)),
            pl.BlockSpec((d, ec), lambda i, e: (0, jnp.maximum(e - 1, 0))),
            rspec, rspec, fspec, fspec,
            pl.BlockSpec((tb, d), lambda i, e: (i, 0), pipeline_mode=pl.Buffered(1)),
        ],
        out_specs=pl.BlockSpec((tb, d), lambda i, e: (i, 0)),
        out_shape=jax.ShapeDtypeStruct((t, d), F32),
        scratch_shapes=[
            pltpu.VMEM((ec, tb), BF16),
            pltpu.VMEM((ec, tb), BF16),
            pltpu.VMEM((d, tb), F32),
        ],
        compiler_params=_params("parallel", "arbitrary"),
        name="peer_experts",
    )(hn, down, upt, w0, n0, w1, r1, h2)


def _largest_tile(total, limit, unit):
    best = unit
    for cand in range(unit, min(total, limit) + 1, unit):
        if total % cand == 0:
            best = cand
    return best


def _pad_cols(a, cols):
    return jnp.pad(a, ((0, 0), (0, cols - a.shape[1])))


def kernel(x, attn_norm, w_in, sb_q_norm, sb_k_norm, conv_w, conv_b, dt_bias, a_log, d_skip,
           sb_out_norm, ssd_out_norm, w_out, ffn_norm, peer_query, peer_sub_keys, peer_down, peer_up):
    b, s, d = x.shape
    t = b * s
    depth = w_in.shape[0]
    sb_width = sb_out_norm.shape[-1]
    sb_heads = sb_width // SB_HEAD_DIM
    ssd_width = ssd_out_norm.shape[-1]
    ssd_heads = dt_bias.shape[-1]
    assert ssd_heads * SSD_HEAD_DIM == ssd_width and ssd_heads <= LANE
    peer_heads, _, n_keys, half = peer_sub_keys.shape[1:]
    n_in = w_in.shape[-1]
    n_in_pad = -(-n_in // LANE) * LANE

    tm_in = _largest_tile(t, 512, SUBLANE)
    tn_in = _largest_tile(n_in_pad, 1152, LANE)
    tq = _largest_tile(s, 256, LANE)
    tm_out = _largest_tile(t, 256, SUBLANE)
    tm_route = _largest_tile(t, 256, LANE)
    tb_peer = _largest_tile(t, 512, LANE)
    rows_per_step = _largest_tile(n_keys, SUBLANE, SUBLANE)

    h2 = x.reshape(t, d)
    for layer in range(depth):
        w_in_l = _pad_cols(w_in[layer], n_in_pad).astype(BF16)
        proj = _in_proj(h2, attn_norm[layer][None, :], w_in_l, tm_in, tn_in).reshape(b, s, n_in_pad)

        attn = _sb_attn(proj, sb_q_norm[layer][None, :], sb_k_norm[layer][None, :], sb_heads, tq)

        cw = conv_w[layer][:, 0, :]
        cb = conv_b[layer][None, :]
        ssd = _ssd(proj, cw[:, :ssd_width], cb[:, :ssd_width], cw[:, ssd_width:], cb[:, ssd_width:],
                   _pad_cols(dt_bias[layer][None, :], LANE), _pad_cols(a_log[layer][None, :], LANE),
                   jnp.repeat(d_skip[layer], SSD_HEAD_DIM)[None, :], ssd_out_norm[layer][None, :],
                   ssd_heads, sb_width)

        w_out_l = w_out[layer].astype(BF16)
        h2, hn = _out_proj(attn.reshape(t, sb_width), ssd.reshape(t, ssd_width), h2,
                           sb_out_norm[layer][None, :], w_out_l[:sb_width], w_out_l[sb_width:],
                           ffn_norm[layer][None, :], tm_out)

        wq_h = peer_query[layer].astype(BF16).reshape(d, peer_heads, 2 * half).transpose(1, 0, 2)
        w0, n0, w1, r1 = _peer_route(hn, wq_h, peer_sub_keys[layer], tm_route)
        h2 = _peer_experts(hn, peer_down[layer].astype(BF16), peer_up[layer].astype(BF16).T,
                           w0, n0, w1, r1, h2, tb_peer, rows_per_step)
    return h2.reshape(b, s, d)
```

```python
import functools
import math

import jax
import jax.numpy as jnp
from jax import lax
from jax.experimental import pallas as pl
from jax.experimental.pallas import tpu as pltpu

F32 = jnp.float32
BF16 = jnp.bfloat16

RMS_EPS = 1e-6
SB_HEAD_DIM = 128
SSD_HEAD_DIM = 64
SSD_GROUPS = 2
SSD_STATE = 128
SSD_CONV = 4
SSD_CHUNK = 128
PEER_TOPK = 16
LANE = 128
SUBLANE = 8
BF16_ROWS = 16
MXU_ACC_ROWS = 512
VMEM_LIMIT = 56 * 1024 * 1024

EXP_ZERO_BOUND = -105.0

NEG_INF = float("-inf")


def _nt_dot(a, b, precision=None):
    return lax.dot_general(a, b, (((1,), (1,)), ((), ())), precision=precision,
                           preferred_element_type=F32)


def _params(*sem):
    return pltpu.CompilerParams(dimension_semantics=sem, vmem_limit_bytes=VMEM_LIMIT)


def _in_proj_kernel(x_ref, g_ref, w_ref, o_ref, xn_ref):
    @pl.when(pl.program_id(1) == 0)
    def _():
        x = x_ref[...]
        inv = lax.rsqrt(jnp.mean(x * x, axis=-1, keepdims=True) + RMS_EPS)
        xn_ref[...] = ((x * inv) * g_ref[...]).astype(BF16)

    for s in range(0, xn_ref.shape[0], MXU_ACC_ROWS):
        rows = pl.ds(s, MXU_ACC_ROWS)
        o_ref[rows, :] = jnp.dot(xn_ref[rows, :], w_ref[...], preferred_element_type=F32)


def _in_proj(x2, gain, w, tm, tn):
    t, d = x2.shape
    n = w.shape[1]
    return pl.pallas_call(
        _in_proj_kernel,
        grid=(t // tm, n // tn),
        in_specs=[
            pl.BlockSpec((tm, d), lambda i, j: (i, 0)),
            pl.BlockSpec((1, d), lambda i, j: (0, 0)),
            pl.BlockSpec((d, tn), lambda i, j: (0, j)),
        ],
        out_specs=pl.BlockSpec((tm, tn), lambda i, j: (i, j)),
        out_shape=jax.ShapeDtypeStruct((t, n), F32),
        scratch_shapes=[pltpu.VMEM((tm, d), BF16)],
        compiler_params=_params("parallel", "arbitrary"),
        name="in_proj",
    )(x2, gain, w)


def _head_rms(x, gain):
    inv = lax.rsqrt(jnp.mean(x * x, axis=-1, keepdims=True) + RMS_EPS)
    return (x * inv) * gain


def _sb_attn_kernel(q_ref, k_ref, v_ref, gq_ref, gk_ref, o_ref, *, tq, scale, chains):
    first = pl.program_id(2) * chains
    qbs = [(_head_rms(q_ref[pl.ds(c * tq, tq), :], gq_ref[...]) * scale).astype(BF16)
           for c in range(chains)]
    row = lax.broadcasted_iota(jnp.int32, (tq, tq), 0)
    col = lax.broadcasted_iota(jnp.int32, (tq, tq), 1)
    causal = col < row
    later_keys = jnp.where(row > col, 1.0, 0.0).astype(BF16)

    def tiles(js, carries, accs, diagonal):
        starts = [pl.multiple_of(j * tq, tq) for j in js]
        kbs = [_head_rms(k_ref[pl.ds(st, tq), :], gk_ref[...]).astype(BF16) for st in starts]
        vbs = [v_ref[pl.ds(st, tq), :].astype(BF16) for st in starts]
        ss = [_nt_dot(qb, kb) for qb, kb in zip(qbs, kbs)]
        softs = [jnp.log1p(jnp.exp(-jnp.abs(s))) for s in ss]
        log_betas = [jnp.minimum(s, 0.0) - soft for s, soft in zip(ss, softs)]
        log_keeps = [lb - s for lb, s in zip(log_betas, ss)]
        if diagonal:
            log_keeps = [jnp.where(causal, lk, 0.0) for lk in log_keeps]
        his = [lk.astype(BF16) for lk in log_keeps]
        los = [(lk - hi.astype(F32)).astype(BF16) for lk, hi in zip(log_keeps, his)]
        laters = [jnp.dot(hi, later_keys, preferred_element_type=F32)
                  + jnp.dot(lo, later_keys, preferred_element_type=F32) for hi, lo in zip(his, los)]
        ws = [jnp.exp(lb + later + carry) for lb, later, carry in zip(log_betas, laters, carries)]
        if diagonal:
            ws = [jnp.where(causal, w, 0.0) for w in ws]
        accs = [acc + jnp.dot(w.astype(BF16), vb, preferred_element_type=F32)
                for acc, w, vb in zip(accs, ws, vbs)]
        carries = [carry + jnp.sum(lk, axis=-1, keepdims=True)
                   for carry, lk in zip(carries, log_keeps)]
        return carries, accs

    carries, accs = tiles([first + c for c in range(chains)],
                          [jnp.zeros((tq, 1), F32)] * chains,
                          [jnp.zeros((tq, SB_HEAD_DIM), F32)] * chains, True)

    def cond(state):
        n, carries = state[0], state[1:1 + chains]
        go = None
        for c in range(chains):
            go_c = jnp.logical_and(first + c - n >= 0, jnp.max(carries[c]) > EXP_ZERO_BOUND)
            go = go_c if go is None else jnp.logical_or(go, go_c)
        return go

    def body(state):
        n, carries, accs = state[0], list(state[1:1 + chains]), list(state[1 + chains:])
        js = [first + c - n for c in range(chains)]
        new_carries, new_accs = tiles([jnp.maximum(j, 0) for j in js], carries, accs, False)
        for c in range(chains):
            exists = js[c] >= 0
            carries[c] = jnp.where(exists, new_carries[c], carries[c])
            accs[c] = jnp.where(exists, new_accs[c], accs[c])
        return (n + 1, *carries, *accs)

    state = lax.while_loop(cond, body, (jnp.int32(1), *carries, *accs))
    for c in range(chains):
        o_ref[pl.ds(c * tq, tq), :] = state[1 + chains + c]


def _sb_attn(proj, gq, gk, heads, tq, chains):
    b, s, _ = proj.shape
    rows = tq * chains
    kernel = functools.partial(_sb_attn_kernel, tq=tq, scale=SB_HEAD_DIM ** -0.5, chains=chains)
    return pl.pallas_call(
        kernel,
        grid=(b, heads, s // rows),
        in_specs=[
            pl.BlockSpec((None, rows, SB_HEAD_DIM), lambda bi, h, i: (bi, i, h)),
            pl.BlockSpec((None, s, SB_HEAD_DIM), lambda bi, h, i: (bi, 0, heads + h)),
            pl.BlockSpec((None, s, SB_HEAD_DIM), lambda bi, h, i: (bi, 0, 2 * heads + h)),
            pl.BlockSpec((1, SB_HEAD_DIM), lambda bi, h, i: (0, 0)),
            pl.BlockSpec((1, SB_HEAD_DIM), lambda bi, h, i: (0, 0)),
        ],
        out_specs=pl.BlockSpec((None, rows, SB_HEAD_DIM), lambda bi, h, i: (bi, i, h)),
        out_shape=jax.ShapeDtypeStruct((b, s, heads * SB_HEAD_DIM), F32),
        compiler_params=_params("parallel", "parallel", "arbitrary"),
        name="sb_attn",
    )(proj, proj, proj, gq, gk)


def _silu(x):
    return x * (1.0 / (1.0 + jnp.exp(-x)))


def _softplus(x):
    return jnp.maximum(x, 0.0) + jnp.log1p(jnp.exp(-jnp.abs(x)))


def _causal_conv(cur, ext_ref, w_ref, b_ref):
    l = cur.shape[0]
    ext_ref[pl.ds(SUBLANE, l), :] = cur
    out = b_ref[...] + w_ref[pl.ds(SSD_CONV - 1, 1), :] * cur
    for tap in range(SSD_CONV - 1):
        shift = SSD_CONV - 1 - tap
        out = out + w_ref[pl.ds(tap, 1), :] * ext_ref[pl.ds(SUBLANE - shift, l), :]
    ext_ref[pl.ds(0, SUBLANE), :] = cur[l - SUBLANE:, :]
    return out


def _ssd_kernel(xs_ref, bc_ref, dt_ref, z_ref, cwx_ref, cbx_ref, cwbc_ref, cbbc_ref,
                dtb_ref, alog_ref, dskip_ref, gain_ref, o_ref,
                extx_ref, extbc_ref, state_ref, *, heads):
    l = SSD_CHUNK
    n = SSD_STATE
    p = SSD_HEAD_DIM
    width = heads * p
    hg = heads // SSD_GROUPS
    gw = hg * p

    @pl.when(pl.program_id(1) == 0)
    def _():
        extx_ref[...] = jnp.zeros_like(extx_ref)
        extbc_ref[...] = jnp.zeros_like(extbc_ref)
        state_ref[...] = jnp.zeros_like(state_ref)

    xs = _silu(_causal_conv(xs_ref[...], extx_ref, cwx_ref, cbx_ref))
    bc = _silu(_causal_conv(bc_ref[...], extbc_ref, cwbc_ref, cbbc_ref))

    dt = _softplus(dt_ref[...] + dtb_ref[...])
    a_dt = dt * (-jnp.exp(alog_ref[...]))
    r = lax.broadcasted_iota(jnp.int32, (l, l), 0)
    c = lax.broadcasted_iota(jnp.int32, (l, l), 1)
    tril = r >= c
    a_cs = jnp.dot(jnp.where(tril, 1.0, 0.0), a_dt, precision=lax.Precision.HIGHEST,
                   preferred_element_type=F32)
    a_cs_t = a_cs.T

    er = lax.broadcasted_iota(jnp.int32, (LANE, width), 0)
    ec = lax.broadcasted_iota(jnp.int32, (LANE, width), 1)
    expand = jnp.where(ec // p == er, 1.0, 0.0)
    dt_e = jnp.dot(dt, expand, precision=lax.Precision.HIGHEST, preferred_element_type=F32)
    a_cs_e = jnp.dot(a_cs, expand, precision=lax.Precision.HIGHEST, preferred_element_type=F32)
    a_end_e = a_cs_e[l - 1:l, :]

    xf = xs * dt_e
    xd = (xf * jnp.exp(a_end_e - a_cs_e)).astype(BF16)
    xfb = xf.astype(BF16)
    decay_out = jnp.exp(a_cs_e)
    chunk_decay = jnp.exp(a_end_e)

    y_parts = []
    for g in range(SSD_GROUPS):
        bm = bc[:, g * n:(g + 1) * n]
        cm = bc[:, (SSD_GROUPS + g) * n:(SSD_GROUPS + g + 1) * n]
        bmb = bm.astype(BF16)
        cmb = cm.astype(BF16)
        cb = _nt_dot(cmb, bmb)
        cols = slice(g * gw, (g + 1) * gw)
        prev = state_ref[:, cols]
        y_off = jnp.dot(cmb, prev.astype(BF16), preferred_element_type=F32) * decay_out[:, cols]
        new_states = jnp.dot(bm.T.astype(BF16), xd[:, cols], preferred_element_type=F32)
        state_ref[:, cols] = prev * chunk_decay[:, cols] + new_states
        for hh in range(hg):
            h = g * hg + hh
            seg = a_cs[:, h:h + 1] - a_cs_t[h:h + 1, :]
            decay = jnp.where(tril, jnp.exp(jnp.where(tril, seg, 0.0)), 0.0)
            scores = (cb * decay).astype(BF16)
            y_diag = jnp.dot(scores, xfb[:, h * p:(h + 1) * p], preferred_element_type=F32)
            y_parts.append(y_diag + y_off[:, hh * p:(hh + 1) * p])
    y = jnp.concatenate(y_parts, axis=-1) + dskip_ref[...] * xs

    y = y * _silu(z_ref[...])
    outs = []
    for g in range(SSD_GROUPS):
        yg = y[:, g * gw:(g + 1) * gw]
        inv = lax.rsqrt(jnp.mean(yg * yg, axis=-1, keepdims=True) + RMS_EPS)
        outs.append(yg * inv)
    o_ref[...] = jnp.concatenate(outs, axis=-1) * gain_ref[...]


def _ssd(proj, cwx, cbx, cwbc, cbbc, dtb, alog, dskip_e, gain, heads, sb_width):
    b, s, _ = proj.shape
    l = SSD_CHUNK
    width = heads * SSD_HEAD_DIM
    bcw = 2 * SSD_GROUPS * SSD_STATE
    z_off = 3 * sb_width
    xs_off = z_off + width
    bc_off = xs_off + width
    dt_off = bc_off + bcw
    assert z_off % width == 0 and xs_off % width == 0 and bc_off % bcw == 0 and dt_off % LANE == 0
    const = lambda shape: pl.BlockSpec(shape, lambda bi, ci: (0, 0))
    kernel = functools.partial(_ssd_kernel, heads=heads)
    return pl.pallas_call(
        kernel,
        grid=(b, s // l),
        in_specs=[
            pl.BlockSpec((None, l, width), lambda bi, ci: (bi, ci, xs_off // width)),
            pl.BlockSpec((None, l, bcw), lambda bi, ci: (bi, ci, bc_off // bcw)),
            pl.BlockSpec((None, l, LANE), lambda bi, ci: (bi, ci, dt_off // LANE)),
            pl.BlockSpec((None, l, width), lambda bi, ci: (bi, ci, z_off // width)),
            const((SSD_CONV, width)), const((1, width)),
            const((SSD_CONV, bcw)), const((1, bcw)),
            const((1, LANE)), const((1, LANE)), const((1, width)), const((1, width)),
        ],
        out_specs=pl.BlockSpec((None, l, width), lambda bi, ci: (bi, ci, 0)),
        out_shape=jax.ShapeDtypeStruct((b, s, width), F32),
        scratch_shapes=[
            pltpu.VMEM((SUBLANE + l, width), F32),
            pltpu.VMEM((SUBLANE + l, bcw), F32),
            pltpu.VMEM((SSD_STATE, width), F32),
        ],
        compiler_params=_params("parallel", "arbitrary"),
        name="ssd",
    )(proj, proj, proj, proj, cwx, cbx, cwbc, cbbc, dtb, alog, dskip_e, gain)


def _out_proj_kernel(attn_ref, ssd_ref, x_ref, ga_ref, wa_ref, ws_ref, gf_ref, wq_ref, keys_ref,
                     h_ref, hn_ref, s_ref):
    a = attn_ref[...]
    inv = lax.rsqrt(jnp.mean(a * a, axis=-1, keepdims=True) + RMS_EPS)
    an = ((a * inv) * ga_ref[...]).astype(BF16)
    h = (x_ref[...]
         + jnp.dot(an, wa_ref[...], preferred_element_type=F32)
         + jnp.dot(ssd_ref[...].astype(BF16), ws_ref[...], preferred_element_type=F32))
    h_ref[...] = h
    inv_h = lax.rsqrt(jnp.mean(h * h, axis=-1, keepdims=True) + RMS_EPS)
    hn = ((h * inv_h) * gf_ref[...]).astype(BF16)
    hn_ref[...] = hn
    q = jnp.dot(hn, wq_ref[...], preferred_element_type=F32)
    half = keys_ref.shape[-1]
    for hc in range(keys_ref.shape[0]):
        s_ref[hc] = _nt_dot(keys_ref[hc], q[:, hc * half:(hc + 1) * half],
                            precision=lax.Precision.HIGHEST)


def _out_proj(attn2, ssd2, x2, ga, wa, ws, gf, wq, keys, tm):
    t, d = x2.shape
    wa_rows, ws_rows = wa.shape[0], ws.shape[0]
    n_hc, n_keys, half = keys.shape
    once = dict(pipeline_mode=pl.Buffered(1))
    return pl.pallas_call(
        _out_proj_kernel,
        grid=(t // tm,),
        in_specs=[
            pl.BlockSpec((tm, wa_rows), lambda i: (i, 0)),
            pl.BlockSpec((tm, ws_rows), lambda i: (i, 0)),
            pl.BlockSpec((tm, d), lambda i: (i, 0)),
            pl.BlockSpec((1, wa_rows), lambda i: (0, 0)),
            pl.BlockSpec((wa_rows, d), lambda i: (0, 0), **once),
            pl.BlockSpec((ws_rows, d), lambda i: (0, 0), **once),
            pl.BlockSpec((1, d), lambda i: (0, 0)),
            pl.BlockSpec((d, n_hc * half), lambda i: (0, 0), **once),
            pl.BlockSpec((n_hc, n_keys, half), lambda i: (0, 0, 0), **once),
        ],
        out_specs=[pl.BlockSpec((tm, d), lambda i: (i, 0)),
                   pl.BlockSpec((tm, d), lambda i: (i, 0)),
                   pl.BlockSpec((n_hc, n_keys, tm), lambda i: (0, 0, i))],
        out_shape=[jax.ShapeDtypeStruct((t, d), F32), jax.ShapeDtypeStruct((t, d), BF16),
                   jax.ShapeDtypeStruct((n_hc, n_keys, t), F32)],
        compiler_params=_params("parallel"),
        name="out_proj",
    )(attn2, ssd2, x2, ga, wa, ws, gf, wq, keys)


def _top_k_rows(scores, k, pos=None):
    work = scores
    rank = jnp.full(scores.shape, float(k), F32)
    tops = []
    for r in range(k):
        m = jnp.max(work, axis=0, keepdims=True)
        hit = work == m
        if pos is not None:
            first = jnp.min(jnp.where(hit, pos, float("inf")), axis=0, keepdims=True)
            hit = pos == first
        rank = jnp.where(hit, float(r), rank)
        work = jnp.where(hit, NEG_INF, work)
        tops.append(m)
    count = jnp.sum(jnp.where(rank < float(k), 1.0, 0.0), axis=0, keepdims=True)
    return tops, rank, count


def _row_iota(rows, cols):
    return lax.broadcasted_iota(jnp.int32, (rows, cols), 0).astype(F32)


def _peer_gate_factors(s0, s1, k, tie_safe):
    assert k == 2 * SUBLANE
    keys, tm = s0.shape
    pos = _row_iota(keys, tm) if tie_safe else None
    tops0, rank0, count0 = _top_k_rows(s0, k, pos)
    tops1, rank1, count1 = _top_k_rows(s1, k, pos)
    t0 = jnp.concatenate(tops0, axis=0)
    t1 = jnp.concatenate(tops1, axis=0)
    i8 = _row_iota(SUBLANE, tm)
    blocks = [tops0[0] + t1, tops0[1] + t1[:SUBLANE]]
    cpos = [_row_iota(k, tm), float(k) + i8]
    for a in range(2, SUBLANE):
        blocks.append(jnp.where(i8 < float(k // (a + 1)), tops0[a] + t1[:SUBLANE], NEG_INF))
        cpos.append(float(a * k) + i8)
    blocks.append(t0[SUBLANE:] + tops1[0])
    cpos.append((i8 + float(SUBLANE)) * float(k))
    cand = jnp.concatenate(blocks, axis=0)
    bests, crank, count2 = _top_k_rows(cand, k, jnp.concatenate(cpos, axis=0) if tie_safe else None)

    sel = jnp.where(crank < float(k), 1.0, 0.0)
    z = jnp.zeros_like(bests[0])
    for r in range(k):
        z = z + jnp.exp(bests[r] - bests[0])
    n0 = jnp.zeros_like(s0)
    row = 0
    for a in range(k):
        rows_a = k if a == 0 else (SUBLANE if a < SUBLANE else 1)
        n_a = jnp.sum(sel[row:row + rows_a, :], axis=0, keepdims=True)
        n0 = jnp.where(rank0 == float(a), n_a, n0)
        row += rows_a
    w0 = (0.5 * jnp.exp(s0 - tops0[0])) / z
    w1 = jnp.exp(s1 - tops1[0])
    most = jnp.max(jnp.maximum(jnp.maximum(count0, count1), count2))
    return w0, n0, w1, rank1, most


def _peer_route_kernel(s_ref, w0_ref, n0_ref, w1_ref, r1_ref):
    k = PEER_TOPK

    def emit(tie_safe):
        most = None
        for g in range(s_ref.shape[-1] // LANE):
            lanes = pl.ds(g * LANE, LANE)
            w0, n0, w1, r1, most_g = _peer_gate_factors(s_ref[0, :, lanes], s_ref[1, :, lanes],
                                                        k, tie_safe)
            w0_ref[:, lanes] = w0
            n0_ref[:, lanes] = n0
            w1_ref[:, lanes] = w1.astype(BF16)
            r1_ref[:, lanes] = r1.astype(BF16)
            most = most_g if most is None else jnp.maximum(most, most_g)
        return most

    most = emit(tie_safe=False)

    @pl.when(most > float(k))
    def _():
        emit(tie_safe=True)


def _peer_route(scores, tm):
    n_hc, n_keys, t = scores.shape
    heads = n_hc // 2
    out32 = jax.ShapeDtypeStruct((heads, n_keys, t), F32)
    out16 = jax.ShapeDtypeStruct((heads, n_keys, t), BF16)
    ospec = pl.BlockSpec((None, n_keys, tm), lambda i, h: (h, 0, i))
    return pl.pallas_call(
        _peer_route_kernel,
        grid=(t // tm, heads),
        in_specs=[pl.BlockSpec((2, n_keys, tm), lambda i, h: (h, 0, i))],
        out_specs=[ospec, ospec, ospec, ospec],
        out_shape=[out32, out32, out16, out16],
        compiler_params=_params("parallel", "parallel"),
        name="peer_route",
    )(scores)


def _peer_experts_kernel(hn_ref, down_ref, upt_ref, w0_ref, n0_ref, w1_ref, r1_ref, h_ref,
                         y_ref, gate_ref, p_ref, acc_ref, *, heads, n_keys, rows_per_step):
    e = pl.program_id(1)
    n_chunks = pl.num_programs(1) - 1
    tb = hn_ref.shape[0]

    @pl.when(e == 0)
    def _():
        acc_ref[...] = jnp.zeros_like(acc_ref)
        p_ref[...] = jnp.zeros_like(p_ref)

    n_slabs = acc_ref.shape[0] // MXU_ACC_ROWS

    def up_project(s):
        rows = pl.ds(s * MXU_ACC_ROWS, MXU_ACC_ROWS)
        acc_ref[rows, :] += jnp.dot(upt_ref[rows, :], p_ref[...], preferred_element_type=F32)

    def build_gates(r):
        for tc in range(tb // LANE):
            lanes = pl.ds(tc * LANE, LANE)
            gate = jnp.zeros((n_keys // BF16_ROWS, BF16_ROWS, LANE), BF16)
            for h in range(heads):
                n_row = jnp.broadcast_to(n0_ref[h, pl.ds(r, 1), lanes], (BF16_ROWS, LANE))
                w_row = jnp.broadcast_to(w0_ref[h, pl.ds(r, 1), lanes], (BF16_ROWS, LANE))
                n_row = n_row.astype(BF16)[None]
                w_row = w_row.astype(BF16)[None]
                live = jnp.maximum(n_row - r1_ref[h, :, :, lanes], 0.0)
                gate = gate + w_row * jnp.minimum(live, w1_ref[h, :, :, lanes])
            gate_ref[pl.ds(r * n_keys, n_keys), lanes] = gate.reshape(n_keys, LANE)

    @pl.when(e < n_chunks)
    def _():
        for s in range(n_slabs):
            for r in range(s * rows_per_step // n_slabs, (s + 1) * rows_per_step // n_slabs):
                build_gates(r)
            up_project(s)

    @pl.when(e != n_chunks)
    def _():
        a = _nt_dot(down_ref[...], hn_ref[...])
        act = a * (1.0 + lax.erf(a * (1.0 / math.sqrt(2.0))))
        p_ref[...] = gate_ref[...] * act.astype(BF16)

    @pl.when(e == n_chunks)
    def _():
        for s in range(n_slabs):
            up_project(s)
        y_ref[...] = h_ref[...] + acc_ref[...].T


def _peer_experts(hn, down, up, w0, n0, w1, r1, h2, tb, rows_per_step):
    t, d = hn.shape
    heads, n_keys, _ = w0.shape
    ec = rows_per_step * n_keys
    n_chunks = down.shape[0] // ec
    last = n_chunks - 1
    upt = up.reshape(n_chunks, ec, d).transpose(0, 2, 1)
    w1 = w1.reshape(heads, n_keys // BF16_ROWS, BF16_ROWS, t)
    r1 = r1.reshape(heads, n_keys // BF16_ROWS, BF16_ROWS, t)
    fspec = pl.BlockSpec((heads, n_keys // BF16_ROWS, BF16_ROWS, tb), lambda i, e: (0, 0, 0, i))
    rspec = pl.BlockSpec((heads, rows_per_step, tb), lambda i, e: (0, jnp.minimum(e, last), i))
    kernel = functools.partial(_peer_experts_kernel, heads=heads, n_keys=n_keys,
                               rows_per_step=rows_per_step)
    return pl.pallas_call(
        kernel,
        grid=(t // tb, n_chunks + 1),
        in_specs=[
            pl.BlockSpec((tb, d), lambda i, e: (i, 0)),
            pl.BlockSpec((ec, d), lambda i, e: (jnp.minimum(e, last), 0)),
            pl.BlockSpec((None, d, ec), lambda i, e: (jnp.maximum(e - 1, 0), 0, 0)),
            rspec, rspec, fspec, fspec,
            pl.BlockSpec((tb, d), lambda i, e: (i, 0), pipeline_mode=pl.Buffered(1)),
        ],
        out_specs=pl.BlockSpec((tb, d), lambda i, e: (i, 0)),
        out_shape=jax.ShapeDtypeStruct((t, d), F32),
        scratch_shapes=[
            pltpu.VMEM((ec, tb), BF16),
            pltpu.VMEM((ec, tb), BF16),
            pltpu.VMEM((d, tb), F32),
        ],
        compiler_params=_params("parallel", "arbitrary"),
        name="peer_experts",
    )(hn, down, upt, w0, n0, w1, r1, h2)


def _largest_tile(total, limit, unit):
    best = unit
    for cand in range(unit, min(total, limit) + 1, unit):
        if total % cand == 0:
            best = cand
    return best


def _pad_cols(a, cols):
    return jnp.pad(a, ((0, 0), (0, cols - a.shape[1])))


def kernel(x, attn_norm, w_in, sb_q_norm, sb_k_norm, conv_w, conv_b, dt_bias, a_log, d_skip,
           sb_out_norm, ssd_out_norm, w_out, ffn_norm, peer_query, peer_sub_keys, peer_down, peer_up):
    b, s, d = x.shape
    t = b * s
    depth = w_in.shape[0]
    sb_width = sb_out_norm.shape[-1]
    sb_heads = sb_width // SB_HEAD_DIM
    ssd_width = ssd_out_norm.shape[-1]
    ssd_heads = dt_bias.shape[-1]
    assert ssd_heads * SSD_HEAD_DIM == ssd_width and ssd_heads <= LANE
    assert t % MXU_ACC_ROWS == 0 and d % MXU_ACC_ROWS == 0
    peer_heads, _, n_keys, half = peer_sub_keys.shape[1:]
    n_in = w_in.shape[-1]
    n_in_pad = -(-n_in // LANE) * LANE

    tm_in = _largest_tile(t, 2 * MXU_ACC_ROWS, MXU_ACC_ROWS)
    tn_in = _largest_tile(n_in_pad, 1152, LANE)
    tq = _largest_tile(s, 256, LANE)
    sb_chains = _largest_tile(s // tq, 4, 1)
    tm_out = _largest_tile(t, 256, SUBLANE)
    tm_route = _largest_tile(t, 256, LANE)
    tb_peer = _largest_tile(t, 512, LANE)
    rows_per_step = _largest_tile(n_keys, SUBLANE, SUBLANE)

    h2 = x.reshape(t, d)
    for layer in range(depth):
        w_in_l = _pad_cols(w_in[layer], n_in_pad).astype(BF16)
        proj = _in_proj(h2, attn_norm[layer][None, :], w_in_l, tm_in, tn_in).reshape(b, s, n_in_pad)

        attn = _sb_attn(proj, sb_q_norm[layer][None, :], sb_k_norm[layer][None, :], sb_heads, tq,
                        sb_chains)

        cw = conv_w[layer][:, 0, :]
        cb = conv_b[layer][None, :]
        ssd = _ssd(proj, cw[:, :ssd_width], cb[:, :ssd_width], cw[:, ssd_width:], cb[:, ssd_width:],
                   _pad_cols(dt_bias[layer][None, :], LANE), _pad_cols(a_log[layer][None, :], LANE),
                   jnp.repeat(d_skip[layer], SSD_HEAD_DIM)[None, :], ssd_out_norm[layer][None, :],
                   ssd_heads, sb_width)

        w_out_l = w_out[layer].astype(BF16)
        h2, hn, scores = _out_proj(attn.reshape(t, sb_width), ssd.reshape(t, ssd_width), h2,
                                   sb_out_norm[layer][None, :], w_out_l[:sb_width], w_out_l[sb_width:],
                                   ffn_norm[layer][None, :], peer_query[layer].astype(BF16),
                                   peer_sub_keys[layer].reshape(2 * peer_heads, n_keys, half), tm_out)

        w0, n0, w1, r1 = _peer_route(scores, tm_route)
        h2 = _peer_experts(hn, peer_down[layer].astype(BF16), peer_up[layer].astype(BF16),
                           w0, n0, w1, r1, h2, tb_peer, rows_per_step)
    return h2.reshape(b, s, d)
```

```python
import functools
import math

import jax
import jax.numpy as jnp
from jax import lax
from jax.experimental import pallas as pl
from jax.experimental.pallas import tpu as pltpu

F32 = jnp.float32
BF16 = jnp.bfloat16

RMS_EPS = 1e-6
SB_HEAD_DIM = 128
SSD_HEAD_DIM = 64
SSD_GROUPS = 2
SSD_STATE = 128
SSD_CONV = 4
SSD_CHUNK = 128
PEER_TOPK = 16
LANE = 128
SUBLANE = 8
BF16_ROWS = 16
MXU_TILE = 256
MXU_ACC_ROWS = 512
VMEM_LIMIT = 56 * 1024 * 1024

EXP_ZERO_BOUND = -105.0

NEG_INF = float("-inf")


def _nt_dot(a, b, precision=None):
    return lax.dot_general(a, b, (((1,), (1,)), ((), ())), precision=precision,
                           preferred_element_type=F32)


def _params(*sem):
    return pltpu.CompilerParams(dimension_semantics=sem, vmem_limit_bytes=VMEM_LIMIT)


def _in_proj_kernel(x_ref, g_ref, w_ref, o_ref, xn_ref):
    @pl.when(pl.program_id(1) == 0)
    def _():
        x = x_ref[...]
        inv = lax.rsqrt(jnp.mean(x * x, axis=-1, keepdims=True) + RMS_EPS)
        xn_ref[...] = ((x * inv) * g_ref[...]).astype(BF16)

    for s in range(0, xn_ref.shape[0], MXU_ACC_ROWS):
        rows = pl.ds(s, MXU_ACC_ROWS)
        o_ref[rows, :] = jnp.dot(xn_ref[rows, :], w_ref[...], preferred_element_type=F32)


def _in_proj(x2, gain, w, tm, tn):
    t, d = x2.shape
    n = w.shape[1]
    return pl.pallas_call(
        _in_proj_kernel,
        grid=(t // tm, n // tn),
        in_specs=[
            pl.BlockSpec((tm, d), lambda i, j: (i, 0)),
            pl.BlockSpec((1, d), lambda i, j: (0, 0)),
            pl.BlockSpec((d, tn), lambda i, j: (0, j)),
        ],
        out_specs=pl.BlockSpec((tm, tn), lambda i, j: (i, j)),
        out_shape=jax.ShapeDtypeStruct((t, n), F32),
        scratch_shapes=[pltpu.VMEM((tm, d), BF16)],
        compiler_params=_params("parallel", "arbitrary"),
        name="in_proj",
    )(x2, gain, w)


def _head_rms(x, gain):
    inv = lax.rsqrt(jnp.mean(x * x, axis=-1, keepdims=True) + RMS_EPS)
    return (x * inv) * gain


def _sb_attn_kernel(q_ref, k_ref, v_ref, gq_ref, gk_ref, o_ref, *, tq, scale, chains):
    first = pl.program_id(2) * chains
    qbs = [(_head_rms(q_ref[pl.ds(c * tq, tq), :], gq_ref[...]) * scale).astype(BF16)
           for c in range(chains)]
    row = lax.broadcasted_iota(jnp.int32, (tq, tq), 0)
    col = lax.broadcasted_iota(jnp.int32, (tq, tq), 1)
    causal = col < row
    later_keys = jnp.where(row > col, 1.0, 0.0).astype(BF16)

    def tiles(js, carries, accs, diagonal):
        starts = [pl.multiple_of(j * tq, tq) for j in js]
        kbs = [_head_rms(k_ref[pl.ds(st, tq), :], gk_ref[...]).astype(BF16) for st in starts]
        vbs = [v_ref[pl.ds(st, tq), :].astype(BF16) for st in starts]
        ss = [_nt_dot(qb, kb) for qb, kb in zip(qbs, kbs)]
        softs = [jnp.log1p(jnp.exp(-jnp.abs(s))) for s in ss]
        log_betas = [jnp.minimum(s, 0.0) - soft for s, soft in zip(ss, softs)]
        log_keeps = [lb - s for lb, s in zip(log_betas, ss)]
        if diagonal:
            log_keeps = [jnp.where(causal, lk, 0.0) for lk in log_keeps]
        his = [lk.astype(BF16) for lk in log_keeps]
        los = [(lk - hi.astype(F32)).astype(BF16) for lk, hi in zip(log_keeps, his)]
        laters = [jnp.dot(hi, later_keys, preferred_element_type=F32)
                  + jnp.dot(lo, later_keys, preferred_element_type=F32) for hi, lo in zip(his, los)]
        ws = [jnp.exp(lb + later + carry) for lb, later, carry in zip(log_betas, laters, carries)]
        if diagonal:
            ws = [jnp.where(causal, w, 0.0) for w in ws]
        accs = [acc + jnp.dot(w.astype(BF16), vb, preferred_element_type=F32)
                for acc, w, vb in zip(accs, ws, vbs)]
        carries = [carry + jnp.sum(lk, axis=-1, keepdims=True)
                   for carry, lk in zip(carries, log_keeps)]
        return carries, accs

    carries, accs = tiles([first + c for c in range(chains)],
                          [jnp.zeros((tq, 1), F32)] * chains,
                          [jnp.zeros((tq, SB_HEAD_DIM), F32)] * chains, True)

    def cond(state):
        n, carries = state[0], state[1:1 + chains]
        go = None
        for c in range(chains):
            go_c = jnp.logical_and(first + c - n >= 0, jnp.max(carries[c]) > EXP_ZERO_BOUND)
            go = go_c if go is None else jnp.logical_or(go, go_c)
        return go

    def body(state):
        n, carries, accs = state[0], list(state[1:1 + chains]), list(state[1 + chains:])
        js = [first + c - n for c in range(chains)]
        new_carries, new_accs = tiles([jnp.maximum(j, 0) for j in js], carries, accs, False)
        for c in range(chains):
            exists = js[c] >= 0
            carries[c] = jnp.where(exists, new_carries[c], carries[c])
            accs[c] = jnp.where(exists, new_accs[c], accs[c])
        return (n + 1, *carries, *accs)

    state = lax.while_loop(cond, body, (jnp.int32(1), *carries, *accs))
    for c in range(chains):
        o_ref[pl.ds(c * tq, tq), :] = state[1 + chains + c]


def _sb_attn(proj, gq, gk, heads, tq, chains):
    b, s, _ = proj.shape
    rows = tq * chains
    kernel = functools.partial(_sb_attn_kernel, tq=tq, scale=SB_HEAD_DIM ** -0.5, chains=chains)
    return pl.pallas_call(
        kernel,
        grid=(b, heads, s // rows),
        in_specs=[
            pl.BlockSpec((None, rows, SB_HEAD_DIM), lambda bi, h, i: (bi, i, h)),
            pl.BlockSpec((None, s, SB_HEAD_DIM), lambda bi, h, i: (bi, 0, heads + h)),
            pl.BlockSpec((None, s, SB_HEAD_DIM), lambda bi, h, i: (bi, 0, 2 * heads + h)),
            pl.BlockSpec((1, SB_HEAD_DIM), lambda bi, h, i: (0, 0)),
            pl.BlockSpec((1, SB_HEAD_DIM), lambda bi, h, i: (0, 0)),
        ],
        out_specs=pl.BlockSpec((None, rows, SB_HEAD_DIM), lambda bi, h, i: (bi, i, h)),
        out_shape=jax.ShapeDtypeStruct((b, s, heads * SB_HEAD_DIM), F32),
        compiler_params=_params("parallel", "parallel", "arbitrary"),
        name="sb_attn",
    )(proj, proj, proj, gq, gk)


def _silu(x):
    return x * (1.0 / (1.0 + jnp.exp(-x)))


def _softplus(x):
    return jnp.maximum(x, 0.0) + jnp.log1p(jnp.exp(-jnp.abs(x)))


def _causal_conv(cur, ext_ref, w_ref, b_ref):
    l = cur.shape[0]
    ext_ref[pl.ds(SUBLANE, l), :] = cur
    out = b_ref[...] + w_ref[pl.ds(SSD_CONV - 1, 1), :] * cur
    for tap in range(SSD_CONV - 1):
        shift = SSD_CONV - 1 - tap
        out = out + w_ref[pl.ds(tap, 1), :] * ext_ref[pl.ds(SUBLANE - shift, l), :]
    ext_ref[pl.ds(0, SUBLANE), :] = cur[l - SUBLANE:, :]
    return out


def _ssd_kernel(xs_ref, bc_ref, dt_ref, z_ref, cwx_ref, cbx_ref, cwbc_ref, cbbc_ref,
                dtb_ref, alog_ref, dskip_ref, gain_ref, o_ref,
                extx_ref, extbc_ref, state_ref, *, heads):
    l = SSD_CHUNK
    n = SSD_STATE
    p = SSD_HEAD_DIM
    width = heads * p
    hg = heads // SSD_GROUPS
    gw = hg * p

    @pl.when(pl.program_id(1) == 0)
    def _():
        extx_ref[...] = jnp.zeros_like(extx_ref)
        extbc_ref[...] = jnp.zeros_like(extbc_ref)
        state_ref[...] = jnp.zeros_like(state_ref)

    xs = _silu(_causal_conv(xs_ref[...], extx_ref, cwx_ref, cbx_ref))
    bc = _silu(_causal_conv(bc_ref[...], extbc_ref, cwbc_ref, cbbc_ref))

    dt = _softplus(dt_ref[...] + dtb_ref[...])
    a_dt = dt * (-jnp.exp(alog_ref[...]))
    r = lax.broadcasted_iota(jnp.int32, (l, l), 0)
    c = lax.broadcasted_iota(jnp.int32, (l, l), 1)
    tril = r >= c
    a_cs = jnp.dot(jnp.where(tril, 1.0, 0.0), a_dt, precision=lax.Precision.HIGHEST,
                   preferred_element_type=F32)
    a_cs_t = a_cs.T

    er = lax.broadcasted_iota(jnp.int32, (LANE, width), 0)
    ec = lax.broadcasted_iota(jnp.int32, (LANE, width), 1)
    expand = jnp.where(ec // p == er, 1.0, 0.0)
    dt_e = jnp.dot(dt, expand, precision=lax.Precision.HIGHEST, preferred_element_type=F32)
    a_cs_e = jnp.dot(a_cs, expand, precision=lax.Precision.HIGHEST, preferred_element_type=F32)
    a_end_e = a_cs_e[l - 1:l, :]

    xf = xs * dt_e
    xd = (xf * jnp.exp(a_end_e - a_cs_e)).astype(BF16)
    xfb = xf.astype(BF16)
    decay_out = jnp.exp(a_cs_e)
    chunk_decay = jnp.exp(a_end_e)

    y_parts = []
    for g in range(SSD_GROUPS):
        bm = bc[:, g * n:(g + 1) * n]
        cm = bc[:, (SSD_GROUPS + g) * n:(SSD_GROUPS + g + 1) * n]
        bmb = bm.astype(BF16)
        cmb = cm.astype(BF16)
        cb = _nt_dot(cmb, bmb)
        cols = slice(g * gw, (g + 1) * gw)
        prev = state_ref[:, cols]
        y_off = jnp.dot(cmb, prev.astype(BF16), preferred_element_type=F32) * decay_out[:, cols]
        new_states = jnp.dot(bm.T.astype(BF16), xd[:, cols], preferred_element_type=F32)
        state_ref[:, cols] = prev * chunk_decay[:, cols] + new_states
        for hh in range(hg):
            h = g * hg + hh
            seg = a_cs[:, h:h + 1] - a_cs_t[h:h + 1, :]
            decay = jnp.where(tril, jnp.exp(jnp.where(tril, seg, 0.0)), 0.0)
            scores = (cb * decay).astype(BF16)
            y_diag = jnp.dot(scores, xfb[:, h * p:(h + 1) * p], preferred_element_type=F32)
            y_parts.append(y_diag + y_off[:, hh * p:(hh + 1) * p])
    y = jnp.concatenate(y_parts, axis=-1) + dskip_ref[...] * xs

    y = y * _silu(z_ref[...])
    outs = []
    for g in range(SSD_GROUPS):
        yg = y[:, g * gw:(g + 1) * gw]
        inv = lax.rsqrt(jnp.mean(yg * yg, axis=-1, keepdims=True) + RMS_EPS)
        outs.append(yg * inv)
    o_ref[...] = jnp.concatenate(outs, axis=-1) * gain_ref[...]


def _ssd(proj, cwx, cbx, cwbc, cbbc, dtb, alog, dskip_e, gain, heads, sb_width):
    b, s, _ = proj.shape
    l = SSD_CHUNK
    width = heads * SSD_HEAD_DIM
    bcw = 2 * SSD_GROUPS * SSD_STATE
    z_off = 3 * sb_width
    xs_off = z_off + width
    bc_off = xs_off + width
    dt_off = bc_off + bcw
    assert z_off % width == 0 and xs_off % width == 0 and bc_off % bcw == 0 and dt_off % LANE == 0
    const = lambda shape: pl.BlockSpec(shape, lambda bi, ci: (0, 0))
    kernel = functools.partial(_ssd_kernel, heads=heads)
    return pl.pallas_call(
        kernel,
        grid=(b, s // l),
        in_specs=[
            pl.BlockSpec((None, l, width), lambda bi, ci: (bi, ci, xs_off // width)),
            pl.BlockSpec((None, l, bcw), lambda bi, ci: (bi, ci, bc_off // bcw)),
            pl.BlockSpec((None, l, LANE), lambda bi, ci: (bi, ci, dt_off // LANE)),
            pl.BlockSpec((None, l, width), lambda bi, ci: (bi, ci, z_off // width)),
            const((SSD_CONV, width)), const((1, width)),
            const((SSD_CONV, bcw)), const((1, bcw)),
            const((1, LANE)), const((1, LANE)), const((1, width)), const((1, width)),
        ],
        out_specs=pl.BlockSpec((None, l, width), lambda bi, ci: (bi, ci, 0)),
        out_shape=jax.ShapeDtypeStruct((b, s, width), F32),
        scratch_shapes=[
            pltpu.VMEM((SUBLANE + l, width), F32),
            pltpu.VMEM((SUBLANE + l, bcw), F32),
            pltpu.VMEM((SSD_STATE, width), F32),
        ],
        compiler_params=_params("parallel", "arbitrary"),
        name="ssd",
    )(proj, proj, proj, proj, cwx, cbx, cwbc, cbbc, dtb, alog, dskip_e, gain)


def _out_proj_kernel(attn_ref, ssd_ref, x_ref, ga_ref, wa_ref, ws_ref, gf_ref, wq_ref, keys_ref,
                     h_ref, hn_ref, s_ref):
    a = attn_ref[...]
    inv = lax.rsqrt(jnp.mean(a * a, axis=-1, keepdims=True) + RMS_EPS)
    an = ((a * inv) * ga_ref[...]).astype(BF16)
    h = (x_ref[...]
         + jnp.dot(an, wa_ref[...], preferred_element_type=F32)
         + jnp.dot(ssd_ref[...].astype(BF16), ws_ref[...], preferred_element_type=F32))
    h_ref[...] = h
    inv_h = lax.rsqrt(jnp.mean(h * h, axis=-1, keepdims=True) + RMS_EPS)
    hn = ((h * inv_h) * gf_ref[...]).astype(BF16)
    hn_ref[...] = hn
    q = jnp.dot(hn, wq_ref[...], preferred_element_type=F32)
    half = keys_ref.shape[-1]
    for hc in range(keys_ref.shape[0]):
        s_ref[hc] = _nt_dot(keys_ref[hc], q[:, hc * half:(hc + 1) * half],
                            precision=lax.Precision.HIGHEST)


def _out_proj(attn2, ssd2, x2, ga, wa, ws, gf, wq, keys, tm):
    t, d = x2.shape
    wa_rows, ws_rows = wa.shape[0], ws.shape[0]
    n_hc, n_keys, half = keys.shape
    once = dict(pipeline_mode=pl.Buffered(1))
    return pl.pallas_call(
        _out_proj_kernel,
        grid=(t // tm,),
        in_specs=[
            pl.BlockSpec((tm, wa_rows), lambda i: (i, 0)),
            pl.BlockSpec((tm, ws_rows), lambda i: (i, 0)),
            pl.BlockSpec((tm, d), lambda i: (i, 0)),
            pl.BlockSpec((1, wa_rows), lambda i: (0, 0)),
            pl.BlockSpec((wa_rows, d), lambda i: (0, 0), **once),
            pl.BlockSpec((ws_rows, d), lambda i: (0, 0), **once),
            pl.BlockSpec((1, d), lambda i: (0, 0)),
            pl.BlockSpec((d, n_hc * half), lambda i: (0, 0), **once),
            pl.BlockSpec((n_hc, n_keys, half), lambda i: (0, 0, 0), **once),
        ],
        out_specs=[pl.BlockSpec((tm, d), lambda i: (i, 0)),
                   pl.BlockSpec((tm, d), lambda i: (i, 0)),
                   pl.BlockSpec((n_hc, n_keys, tm), lambda i: (0, 0, i))],
        out_shape=[jax.ShapeDtypeStruct((t, d), F32), jax.ShapeDtypeStruct((t, d), BF16),
                   jax.ShapeDtypeStruct((n_hc, n_keys, t), F32)],
        compiler_params=_params("parallel"),
        name="out_proj",
    )(attn2, ssd2, x2, ga, wa, ws, gf, wq, keys)


def _top_k_rows(scores, k, pos=None):
    work = scores
    rank = jnp.full(scores.shape, float(k), F32)
    tops = []
    for r in range(k):
        m = jnp.max(work, axis=0, keepdims=True)
        hit = work == m
        if pos is not None:
            first = jnp.min(jnp.where(hit, pos, float("inf")), axis=0, keepdims=True)
            hit = pos == first
        rank = jnp.where(hit, float(r), rank)
        work = jnp.where(hit, NEG_INF, work)
        tops.append(m)
    count = jnp.sum(jnp.where(rank < float(k), 1.0, 0.0), axis=0, keepdims=True)
    return tops, rank, count


def _row_iota(rows, cols):
    return lax.broadcasted_iota(jnp.int32, (rows, cols), 0).astype(F32)


def _peer_gate_factors(s0, s1, k, tie_safe):
    assert k == 2 * SUBLANE
    keys, tm = s0.shape
    pos = _row_iota(keys, tm) if tie_safe else None
    tops0, rank0, count0 = _top_k_rows(s0, k, pos)
    tops1, rank1, count1 = _top_k_rows(s1, k, pos)
    t0 = jnp.concatenate(tops0, axis=0)
    t1 = jnp.concatenate(tops1, axis=0)
    i8 = _row_iota(SUBLANE, tm)
    blocks = [tops0[0] + t1, tops0[1] + t1[:SUBLANE]]
    cpos = [_row_iota(k, tm), float(k) + i8]
    for a in range(2, SUBLANE):
        blocks.append(jnp.where(i8 < float(k // (a + 1)), tops0[a] + t1[:SUBLANE], NEG_INF))
        cpos.append(float(a * k) + i8)
    blocks.append(t0[SUBLANE:] + tops1[0])
    cpos.append((i8 + float(SUBLANE)) * float(k))
    cand = jnp.concatenate(blocks, axis=0)
    bests, crank, count2 = _top_k_rows(cand, k, jnp.concatenate(cpos, axis=0) if tie_safe else None)

    sel = jnp.where(crank < float(k), 1.0, 0.0)
    z = jnp.zeros_like(bests[0])
    for r in range(k):
        z = z + jnp.exp(bests[r] - bests[0])
    n0 = jnp.zeros_like(s0)
    row = 0
    for a in range(k):
        rows_a = k if a == 0 else (SUBLANE if a < SUBLANE else 1)
        n_a = jnp.sum(sel[row:row + rows_a, :], axis=0, keepdims=True)
        n0 = jnp.where(rank0 == float(a), n_a, n0)
        row += rows_a
    w0 = (0.5 * jnp.exp(s0 - tops0[0])) / z
    w1 = jnp.exp(s1 - tops1[0])
    most = jnp.max(jnp.maximum(jnp.maximum(count0, count1), count2))
    return w0, n0, w1, rank1, most


def _peer_route_kernel(s_ref, w0_ref, n0_ref, w1_ref, r1_ref):
    k = PEER_TOPK

    def emit(tie_safe):
        most = None
        for g in range(s_ref.shape[-1] // LANE):
            lanes = pl.ds(g * LANE, LANE)
            w0, n0, w1, r1, most_g = _peer_gate_factors(s_ref[0, :, lanes], s_ref[1, :, lanes],
                                                        k, tie_safe)
            w0_ref[:, lanes] = w0
            n0_ref[:, lanes] = n0
            w1_ref[:, lanes] = w1.astype(BF16)
            r1_ref[:, lanes] = r1.astype(BF16)
            most = most_g if most is None else jnp.maximum(most, most_g)
        return most

    most = emit(tie_safe=False)

    @pl.when(most > float(k))
    def _():
        emit(tie_safe=True)


def _peer_route(scores, tm):
    n_hc, n_keys, t = scores.shape
    heads = n_hc // 2
    out32 = jax.ShapeDtypeStruct((heads, n_keys, t), F32)
    out16 = jax.ShapeDtypeStruct((heads, n_keys, t), BF16)
    ospec = pl.BlockSpec((None, n_keys, tm), lambda i, h: (h, 0, i))
    return pl.pallas_call(
        _peer_route_kernel,
        grid=(t // tm, heads),
        in_specs=[pl.BlockSpec((2, n_keys, tm), lambda i, h: (h, 0, i))],
        out_specs=[ospec, ospec, ospec, ospec],
        out_shape=[out32, out32, out16, out16],
        compiler_params=_params("parallel", "parallel"),
        name="peer_route",
    )(scores)


def _peer_experts_kernel(hn_ref, down_ref, upt_prev_ref, upt_ref, w0_ref, n0_ref, w1_ref, r1_ref,
                         h_ref, y_ref, pa_ref, pb_ref, acc_ref, *, heads, n_keys):
    g = pl.program_id(1)
    n_steps = pl.num_programs(1) - 1
    tb = hn_ref.shape[0]
    half = pa_ref.shape[0]
    rows_per_chunk = half // n_keys

    @pl.when(g == 0)
    def _():
        acc_ref[...] = jnp.zeros_like(acc_ref)
        pb_ref[...] = jnp.zeros_like(pb_ref)

    def activate(c, p_ref):
        a = _nt_dot(down_ref[pl.ds(c * half, half), :], hn_ref[...])
        act = (a * (1.0 + lax.erf(a * (1.0 / math.sqrt(2.0))))).astype(BF16)
        for rr in range(rows_per_chunk):
            r = c * rows_per_chunk + rr
            for tc in range(tb // LANE):
                lanes = slice(tc * LANE, (tc + 1) * LANE)
                gate = jnp.zeros((n_keys // BF16_ROWS, BF16_ROWS, LANE), BF16)
                for h in range(heads):
                    n_row = jnp.broadcast_to(n0_ref[h, pl.ds(r, 1), lanes], (BF16_ROWS, LANE))
                    w_row = jnp.broadcast_to(w0_ref[h, pl.ds(r, 1), lanes], (BF16_ROWS, LANE))
                    n_row = n_row.astype(BF16)[None]
                    w_row = w_row.astype(BF16)[None]
                    live = jnp.maximum(n_row - r1_ref[h, :, :, lanes], 0.0)
                    gate = gate + w_row * jnp.minimum(live, w1_ref[h, :, :, lanes])
                rows = slice(rr * n_keys, (rr + 1) * n_keys)
                p_ref[rows, lanes] = gate.reshape(n_keys, LANE) * act[rows, lanes]

    def up_project(w_ref, p_ref):
        for s in range(0, acc_ref.shape[0], MXU_ACC_ROWS):
            rows = pl.ds(s, MXU_ACC_ROWS)
            acc_ref[rows, :] += jnp.dot(w_ref[rows, :], p_ref[...], preferred_element_type=F32)

    @pl.when(g < n_steps)
    def _():
        up_project(upt_prev_ref, pb_ref)
        activate(0, pa_ref)
        activate(1, pb_ref)
        up_project(upt_ref, pa_ref)

    @pl.when(g == n_steps)
    def _():
        up_project(upt_prev_ref, pb_ref)
        y_ref[...] = h_ref[...] + acc_ref[...].T


def _peer_experts(hn, down, up, w0, n0, w1, r1, h2, tb, rows_per_step):
    t, d = hn.shape
    heads, n_keys, _ = w0.shape
    assert rows_per_step % 2 == 0
    half = rows_per_step * n_keys // 2
    n_chunks = down.shape[0] // half
    n_steps = n_chunks // 2
    upt = up.reshape(n_chunks, half, d).transpose(0, 2, 1)
    w1 = w1.reshape(heads, n_keys // BF16_ROWS, BF16_ROWS, t)
    r1 = r1.reshape(heads, n_keys // BF16_ROWS, BF16_ROWS, t)
    fspec = pl.BlockSpec((heads, n_keys // BF16_ROWS, BF16_ROWS, tb), lambda i, g: (0, 0, 0, i))
    rspec = pl.BlockSpec((heads, rows_per_step, tb),
                         lambda i, g: (0, jnp.minimum(g, n_steps - 1), i))
    kernel = functools.partial(_peer_experts_kernel, heads=heads, n_keys=n_keys)
    return pl.pallas_call(
        kernel,
        grid=(t // tb, n_steps + 1),
        in_specs=[
            pl.BlockSpec((tb, d), lambda i, g: (i, 0)),
            pl.BlockSpec((2 * half, d), lambda i, g: (jnp.minimum(g, n_steps - 1), 0)),
            pl.BlockSpec((None, d, half), lambda i, g: (jnp.maximum(2 * g - 1, 0), 0, 0)),
            pl.BlockSpec((None, d, half), lambda i, g: (jnp.minimum(2 * g, n_chunks - 1), 0, 0)),
            rspec, rspec, fspec, fspec,
            pl.BlockSpec((tb, d), lambda i, g: (i, 0), pipeline_mode=pl.Buffered(1)),
        ],
        out_specs=pl.BlockSpec((tb, d), lambda i, g: (i, 0)),
        out_shape=jax.ShapeDtypeStruct((t, d), F32),
        scratch_shapes=[
            pltpu.VMEM((half, tb), BF16),
            pltpu.VMEM((half, tb), BF16),
            pltpu.VMEM((d, tb), F32),
        ],
        compiler_params=_params("parallel", "arbitrary"),
        name="peer_experts",
    )(hn, down, upt, upt, w0, n0, w1, r1, h2)


def _largest_tile(total, limit, unit):
    best = unit
    for cand in range(unit, min(total, limit) + 1, unit):
        if total % cand == 0:
            best = cand
    return best


def _pad_cols(a, cols):
    return jnp.pad(a, ((0, 0), (0, cols - a.shape[1])))


def kernel(x, attn_norm, w_in, sb_q_norm, sb_k_norm, conv_w, conv_b, dt_bias, a_log, d_skip,
           sb_out_norm, ssd_out_norm, w_out, ffn_norm, peer_query, peer_sub_keys, peer_down, peer_up):
    b, s, d = x.shape
    t = b * s
    depth = w_in.shape[0]
    sb_width = sb_out_norm.shape[-1]
    sb_heads = sb_width // SB_HEAD_DIM
    ssd_width = ssd_out_norm.shape[-1]
    ssd_heads = dt_bias.shape[-1]
    assert ssd_heads * SSD_HEAD_DIM == ssd_width and ssd_heads <= LANE
    assert t % MXU_ACC_ROWS == 0 and d % MXU_ACC_ROWS == 0
    peer_heads, _, n_keys, half = peer_sub_keys.shape[1:]
    n_in = w_in.shape[-1]
    n_in_pad = -(-n_in // LANE) * LANE

    tm_in = _largest_tile(t, 2 * MXU_ACC_ROWS, MXU_ACC_ROWS)
    tn_in = _largest_tile(n_in_pad, 1152, LANE)
    tq = _largest_tile(s, 256, LANE)
    sb_chains = _largest_tile(s // tq, 4, 1)
    tm_out = _largest_tile(t, 256, SUBLANE)
    tm_route = _largest_tile(t, 256, LANE)
    tb_peer = _largest_tile(t, 512, LANE)
    rows_per_step = _largest_tile(n_keys, SUBLANE, SUBLANE)

    h2 = x.reshape(t, d)
    for layer in range(depth):
        w_in_l = _pad_cols(w_in[layer], n_in_pad).astype(BF16)
        proj = _in_proj(h2, attn_norm[layer][None, :], w_in_l, tm_in, tn_in).reshape(b, s, n_in_pad)

        attn = _sb_attn(proj, sb_q_norm[layer][None, :], sb_k_norm[layer][None, :], sb_heads, tq,
                        sb_chains)

        cw = conv_w[layer][:, 0, :]
        cb = conv_b[layer][None, :]
        ssd = _ssd(proj, cw[:, :ssd_width], cb[:, :ssd_width], cw[:, ssd_width:], cb[:, ssd_width:],
                   _pad_cols(dt_bias[layer][None, :], LANE), _pad_cols(a_log[layer][None, :], LANE),
                   jnp.repeat(d_skip[layer], SSD_HEAD_DIM)[None, :], ssd_out_norm[layer][None, :],
                   ssd_heads, sb_width)

        w_out_l = w_out[layer].astype(BF16)
        h2, hn, scores = _out_proj(attn.reshape(t, sb_width), ssd.reshape(t, ssd_width), h2,
                                   sb_out_norm[layer][None, :], w_out_l[:sb_width], w_out_l[sb_width:],
                                   ffn_norm[layer][None, :], peer_query[layer].astype(BF16),
                                   peer_sub_keys[layer].reshape(2 * peer_heads, n_keys, half), tm_out)

        w0, n0, w1, r1 = _peer_route(scores, tm_route)
        h2 = _peer_experts(hn, peer_down[layer].astype(BF16), peer_up[layer].astype(BF16),
                           w0, n0, w1, r1, h2, tb_peer, rows_per_step)
    return h2.reshape(b, s, d)
```

```python
import functools
import math

import jax
import jax.numpy as jnp
from jax import lax
from jax.experimental import pallas as pl
from jax.experimental.pallas import tpu as pltpu

F32 = jnp.float32
BF16 = jnp.bfloat16

RMS_EPS = 1e-6
SB_HEAD_DIM = 128
SSD_HEAD_DIM = 64
SSD_GROUPS = 2
SSD_STATE = 128
SSD_CONV = 4
SSD_CHUNK = 128
PEER_TOPK = 16
LANE = 128
SUBLANE = 8
BF16_ROWS = 16
MXU_TILE = 256
MXU_ACC_ROWS = 512
VMEM_LIMIT = 56 * 1024 * 1024

EXP_ZERO_BOUND = -105.0

NEG_INF = float("-inf")


def _nt_dot(a, b, precision=None):
    return lax.dot_general(a, b, (((1,), (1,)), ((), ())), precision=precision,
                           preferred_element_type=F32)


def _params(*sem):
    return pltpu.CompilerParams(dimension_semantics=sem, vmem_limit_bytes=VMEM_LIMIT)


def _in_proj_kernel(x_ref, g_ref, w_ref, o_ref, xn_ref):
    @pl.when(pl.program_id(1) == 0)
    def _():
        x = x_ref[...]
        inv = lax.rsqrt(jnp.mean(x * x, axis=-1, keepdims=True) + RMS_EPS)
        xn_ref[...] = ((x * inv) * g_ref[...]).astype(BF16)

    for s in range(0, xn_ref.shape[0], MXU_ACC_ROWS):
        rows = pl.ds(s, MXU_ACC_ROWS)
        o_ref[rows, :] = jnp.dot(xn_ref[rows, :], w_ref[...], preferred_element_type=F32)


def _in_proj(x2, gain, w, tm, tn):
    t, d = x2.shape
    n = w.shape[1]
    return pl.pallas_call(
        _in_proj_kernel,
        grid=(t // tm, n // tn),
        in_specs=[
            pl.BlockSpec((tm, d), lambda i, j: (i, 0)),
            pl.BlockSpec((1, d), lambda i, j: (0, 0)),
            pl.BlockSpec((d, tn), lambda i, j: (0, j)),
        ],
        out_specs=pl.BlockSpec((tm, tn), lambda i, j: (i, j)),
        out_shape=jax.ShapeDtypeStruct((t, n), F32),
        scratch_shapes=[pltpu.VMEM((tm, d), BF16)],
        compiler_params=_params("parallel", "arbitrary"),
        name="in_proj",
    )(x2, gain, w)


def _head_rms(x, gain):
    inv = lax.rsqrt(jnp.mean(x * x, axis=-1, keepdims=True) + RMS_EPS)
    return (x * inv) * gain


def _sb_attn_kernel(q_ref, k_ref, v_ref, gq_ref, gk_ref, o_ref, *, tq, scale, chains):
    first = pl.program_id(2) * chains
    qbs = [(_head_rms(q_ref[pl.ds(c * tq, tq), :], gq_ref[...]) * scale).astype(BF16)
           for c in range(chains)]
    row = lax.broadcasted_iota(jnp.int32, (tq, tq), 0)
    col = lax.broadcasted_iota(jnp.int32, (tq, tq), 1)
    causal = col < row
    later_keys = jnp.where(row > col, 1.0, 0.0).astype(BF16)

    def tiles(js, carries, accs, diagonal):
        starts = [pl.multiple_of(j * tq, tq) for j in js]
        kbs = [_head_rms(k_ref[pl.ds(st, tq), :], gk_ref[...]).astype(BF16) for st in starts]
        vbs = [v_ref[pl.ds(st, tq), :].astype(BF16) for st in starts]
        ss = [_nt_dot(qb, kb) for qb, kb in zip(qbs, kbs)]
        softs = [jnp.log1p(jnp.exp(-jnp.abs(s))) for s in ss]
        log_betas = [jnp.minimum(s, 0.0) - soft for s, soft in zip(ss, softs)]
        log_keeps = [lb - s for lb, s in zip(log_betas, ss)]
        if diagonal:
            log_keeps = [jnp.where(causal, lk, 0.0) for lk in log_keeps]
        his = [lk.astype(BF16) for lk in log_keeps]
        los = [(lk - hi.astype(F32)).astype(BF16) for lk, hi in zip(log_keeps, his)]
        laters = [jnp.dot(hi, later_keys, preferred_element_type=F32)
                  + jnp.dot(lo, later_keys, preferred_element_type=F32) for hi, lo in zip(his, los)]
        ws = [jnp.exp(lb + later + carry) for lb, later, carry in zip(log_betas, laters, carries)]
        if diagonal:
            ws = [jnp.where(causal, w, 0.0) for w in ws]
        accs = [acc + jnp.dot(w.astype(BF16), vb, preferred_element_type=F32)
                for acc, w, vb in zip(accs, ws, vbs)]
        carries = [carry + jnp.sum(lk, axis=-1, keepdims=True)
                   for carry, lk in zip(carries, log_keeps)]
        return carries, accs

    carries, accs = tiles([first + c for c in range(chains)],
                          [jnp.zeros((tq, 1), F32)] * chains,
                          [jnp.zeros((tq, SB_HEAD_DIM), F32)] * chains, True)

    def cond(state):
        n, carries = state[0], state[1:1 + chains]
        go = None
        for c in range(chains):
            go_c = jnp.logical_and(first + c - n >= 0, jnp.max(carries[c]) > EXP_ZERO_BOUND)
            go = go_c if go is None else jnp.logical_or(go, go_c)
        return go

    def body(state):
        n, carries, accs = state[0], list(state[1:1 + chains]), list(state[1 + chains:])
        js = [first + c - n for c in range(chains)]
        new_carries, new_accs = tiles([jnp.maximum(j, 0) for j in js], carries, accs, False)
        for c in range(chains):
            exists = js[c] >= 0
            carries[c] = jnp.where(exists, new_carries[c], carries[c])
            accs[c] = jnp.where(exists, new_accs[c], accs[c])
        return (n + 1, *carries, *accs)

    state = lax.while_loop(cond, body, (jnp.int32(1), *carries, *accs))
    for c in range(chains):
        o_ref[pl.ds(c * tq, tq), :] = state[1 + chains + c]


def _sb_attn(proj, gq, gk, heads, tq, chains):
    b, s, _ = proj.shape
    rows = tq * chains
    kernel = functools.partial(_sb_attn_kernel, tq=tq, scale=SB_HEAD_DIM ** -0.5, chains=chains)
    return pl.pallas_call(
        kernel,
        grid=(b, heads, s // rows),
        in_specs=[
            pl.BlockSpec((None, rows, SB_HEAD_DIM), lambda bi, h, i: (bi, i, h)),
            pl.BlockSpec((None, s, SB_HEAD_DIM), lambda bi, h, i: (bi, 0, heads + h)),
            pl.BlockSpec((None, s, SB_HEAD_DIM), lambda bi, h, i: (bi, 0, 2 * heads + h)),
            pl.BlockSpec((1, SB_HEAD_DIM), lambda bi, h, i: (0, 0)),
            pl.BlockSpec((1, SB_HEAD_DIM), lambda bi, h, i: (0, 0)),
        ],
        out_specs=pl.BlockSpec((None, rows, SB_HEAD_DIM), lambda bi, h, i: (bi, i, h)),
        out_shape=jax.ShapeDtypeStruct((b, s, heads * SB_HEAD_DIM), F32),
        compiler_params=_params("parallel", "parallel", "arbitrary"),
        name="sb_attn",
    )(proj, proj, proj, gq, gk)


def _silu(x):
    return x * (1.0 / (1.0 + jnp.exp(-x)))


def _softplus(x):
    return jnp.maximum(x, 0.0) + jnp.log1p(jnp.exp(-jnp.abs(x)))


def _causal_conv(cur, ext_ref, w_ref, b_ref):
    l = cur.shape[0]
    ext_ref[pl.ds(SUBLANE, l), :] = cur
    out = b_ref[...] + w_ref[pl.ds(SSD_CONV - 1, 1), :] * cur
    for tap in range(SSD_CONV - 1):
        shift = SSD_CONV - 1 - tap
        out = out + w_ref[pl.ds(tap, 1), :] * ext_ref[pl.ds(SUBLANE - shift, l), :]
    ext_ref[pl.ds(0, SUBLANE), :] = cur[l - SUBLANE:, :]
    return out


def _ssd_kernel(xs_ref, bc_ref, dt_ref, z_ref, cwx_ref, cbx_ref, cwbc_ref, cbbc_ref,
                dtb_ref, alog_ref, dskip_ref, gain_ref, o_ref,
                extx_ref, extbc_ref, state_ref, *, heads):
    l = SSD_CHUNK
    n = SSD_STATE
    p = SSD_HEAD_DIM
    width = heads * p
    hg = heads // SSD_GROUPS
    gw = hg * p

    @pl.when(pl.program_id(1) == 0)
    def _():
        extx_ref[...] = jnp.zeros_like(extx_ref)
        extbc_ref[...] = jnp.zeros_like(extbc_ref)
        state_ref[...] = jnp.zeros_like(state_ref)

    xs = _silu(_causal_conv(xs_ref[...], extx_ref, cwx_ref, cbx_ref))
    bc = _silu(_causal_conv(bc_ref[...], extbc_ref, cwbc_ref, cbbc_ref))

    dt = _softplus(dt_ref[...] + dtb_ref[...])
    a_dt = dt * (-jnp.exp(alog_ref[...]))
    r = lax.broadcasted_iota(jnp.int32, (l, l), 0)
    c = lax.broadcasted_iota(jnp.int32, (l, l), 1)
    tril = r >= c
    a_cs = jnp.dot(jnp.where(tril, 1.0, 0.0), a_dt, precision=lax.Precision.HIGHEST,
                   preferred_element_type=F32)
    a_cs_t = a_cs.T

    er = lax.broadcasted_iota(jnp.int32, (LANE, width), 0)
    ec = lax.broadcasted_iota(jnp.int32, (LANE, width), 1)
    expand = jnp.where(ec // p == er, 1.0, 0.0)
    dt_e = jnp.dot(dt, expand, precision=lax.Precision.HIGHEST, preferred_element_type=F32)
    a_cs_e = jnp.dot(a_cs, expand, precision=lax.Precision.HIGHEST, preferred_element_type=F32)
    a_end_e = a_cs_e[l - 1:l, :]

    xf = xs * dt_e
    xd = (xf * jnp.exp(a_end_e - a_cs_e)).astype(BF16)
    xfb = xf.astype(BF16)
    decay_out = jnp.exp(a_cs_e)
    chunk_decay = jnp.exp(a_end_e)

    y_parts = []
    for g in range(SSD_GROUPS):
        bm = bc[:, g * n:(g + 1) * n]
        cm = bc[:, (SSD_GROUPS + g) * n:(SSD_GROUPS + g + 1) * n]
        bmb = bm.astype(BF16)
        cmb = cm.astype(BF16)
        cb = _nt_dot(cmb, bmb)
        cols = slice(g * gw, (g + 1) * gw)
        prev = state_ref[:, cols]
        y_off = jnp.dot(cmb, prev.astype(BF16), preferred_element_type=F32) * decay_out[:, cols]
        new_states = jnp.dot(bm.T.astype(BF16), xd[:, cols], preferred_element_type=F32)
        state_ref[:, cols] = prev * chunk_decay[:, cols] + new_states
        for hh in range(hg):
            h = g * hg + hh
            seg = a_cs[:, h:h + 1] - a_cs_t[h:h + 1, :]
            decay = jnp.where(tril, jnp.exp(jnp.where(tril, seg, 0.0)), 0.0)
            scores = (cb * decay).astype(BF16)
            y_diag = jnp.dot(scores, xfb[:, h * p:(h + 1) * p], preferred_element_type=F32)
            y_parts.append(y_diag + y_off[:, hh * p:(hh + 1) * p])
    y = jnp.concatenate(y_parts, axis=-1) + dskip_ref[...] * xs

    y = y * _silu(z_ref[...])
    outs = []
    for g in range(SSD_GROUPS):
        yg = y[:, g * gw:(g + 1) * gw]
        inv = lax.rsqrt(jnp.mean(yg * yg, axis=-1, keepdims=True) + RMS_EPS)
        outs.append(yg * inv)
    o_ref[...] = jnp.concatenate(outs, axis=-1) * gain_ref[...]


def _ssd(proj, cwx, cbx, cwbc, cbbc, dtb, alog, dskip_e, gain, heads, sb_width):
    b, s, _ = proj.shape
    l = SSD_CHUNK
    width = heads * SSD_HEAD_DIM
    bcw = 2 * SSD_GROUPS * SSD_STATE
    z_off = 3 * sb_width
    xs_off = z_off + width
    bc_off = xs_off + width
    dt_off = bc_off + bcw
    assert z_off % width == 0 and xs_off % width == 0 and bc_off % bcw == 0 and dt_off % LANE == 0
    const = lambda shape: pl.BlockSpec(shape, lambda bi, ci: (0, 0))
    kernel = functools.partial(_ssd_kernel, heads=heads)
    return pl.pallas_call(
        kernel,
        grid=(b, s // l),
        in_specs=[
            pl.BlockSpec((None, l, width), lambda bi, ci: (bi, ci, xs_off // width)),
            pl.BlockSpec((None, l, bcw), lambda bi, ci: (bi, ci, bc_off // bcw)),
            pl.BlockSpec((None, l, LANE), lambda bi, ci: (bi, ci, dt_off // LANE)),
            pl.BlockSpec((None, l, width), lambda bi, ci: (bi, ci, z_off // width)),
            const((SSD_CONV, width)), const((1, width)),
            const((SSD_CONV, bcw)), const((1, bcw)),
            const((1, LANE)), const((1, LANE)), const((1, width)), const((1, width)),
        ],
        out_specs=pl.BlockSpec((None, l, width), lambda bi, ci: (bi, ci, 0)),
        out_shape=jax.ShapeDtypeStruct((b, s, width), F32),
        scratch_shapes=[
            pltpu.VMEM((SUBLANE + l, width), F32),
            pltpu.VMEM((SUBLANE + l, bcw), F32),
            pltpu.VMEM((SSD_STATE, width), F32),
        ],
        compiler_params=_params("parallel", "arbitrary"),
        name="ssd",
    )(proj, proj, proj, proj, cwx, cbx, cwbc, cbbc, dtb, alog, dskip_e, gain)


def _out_proj_kernel(attn_ref, ssd_ref, x_ref, ga_ref, wa_ref, ws_ref, gf_ref, wq_ref, keys_ref,
                     h_ref, hn_ref, s_ref):
    a = attn_ref[...]
    inv = lax.rsqrt(jnp.mean(a * a, axis=-1, keepdims=True) + RMS_EPS)
    an = ((a * inv) * ga_ref[...]).astype(BF16)
    h = (x_ref[...]
         + jnp.dot(an, wa_ref[...], preferred_element_type=F32)
         + jnp.dot(ssd_ref[...].astype(BF16), ws_ref[...], preferred_element_type=F32))
    h_ref[...] = h
    inv_h = lax.rsqrt(jnp.mean(h * h, axis=-1, keepdims=True) + RMS_EPS)
    hn = ((h * inv_h) * gf_ref[...]).astype(BF16)
    hn_ref[...] = hn
    q = jnp.dot(hn, wq_ref[...], preferred_element_type=F32)
    half = keys_ref.shape[-1]
    for hc in range(keys_ref.shape[0]):
        s_ref[hc] = _nt_dot(keys_ref[hc], q[:, hc * half:(hc + 1) * half],
                            precision=lax.Precision.HIGHEST)


def _out_proj(attn2, ssd2, x2, ga, wa, ws, gf, wq, keys, tm):
    t, d = x2.shape
    wa_rows, ws_rows = wa.shape[0], ws.shape[0]
    n_hc, n_keys, half = keys.shape
    once = dict(pipeline_mode=pl.Buffered(1))
    return pl.pallas_call(
        _out_proj_kernel,
        grid=(t // tm,),
        in_specs=[
            pl.BlockSpec((tm, wa_rows), lambda i: (i, 0)),
            pl.BlockSpec((tm, ws_rows), lambda i: (i, 0)),
            pl.BlockSpec((tm, d), lambda i: (i, 0)),
            pl.BlockSpec((1, wa_rows), lambda i: (0, 0)),
            pl.BlockSpec((wa_rows, d), lambda i: (0, 0), **once),
            pl.BlockSpec((ws_rows, d), lambda i: (0, 0), **once),
            pl.BlockSpec((1, d), lambda i: (0, 0)),
            pl.BlockSpec((d, n_hc * half), lambda i: (0, 0), **once),
            pl.BlockSpec((n_hc, n_keys, half), lambda i: (0, 0, 0), **once),
        ],
        out_specs=[pl.BlockSpec((tm, d), lambda i: (i, 0)),
                   pl.BlockSpec((tm, d), lambda i: (i, 0)),
                   pl.BlockSpec((n_hc, n_keys, tm), lambda i: (0, 0, i))],
        out_shape=[jax.ShapeDtypeStruct((t, d), F32), jax.ShapeDtypeStruct((t, d), BF16),
                   jax.ShapeDtypeStruct((n_hc, n_keys, t), F32)],
        compiler_params=_params("parallel"),
        name="out_proj",
    )(attn2, ssd2, x2, ga, wa, ws, gf, wq, keys)


def _top_k_rows(scores, k, pos=None):
    work = scores
    rank = jnp.full(scores.shape, float(k), F32)
    tops = []
    for r in range(k):
        m = jnp.max(work, axis=0, keepdims=True)
        hit = work == m
        if pos is not None:
            first = jnp.min(jnp.where(hit, pos, float("inf")), axis=0, keepdims=True)
            hit = pos == first
        rank = jnp.where(hit, float(r), rank)
        work = jnp.where(hit, NEG_INF, work)
        tops.append(m)
    count = jnp.sum(jnp.where(rank < float(k), 1.0, 0.0), axis=0, keepdims=True)
    return tops, rank, count


def _row_iota(rows, cols):
    return lax.broadcasted_iota(jnp.int32, (rows, cols), 0).astype(F32)


def _peer_gate_factors(s0, s1, k, tie_safe):
    assert k == 2 * SUBLANE
    keys, tm = s0.shape
    pos = _row_iota(keys, tm) if tie_safe else None
    tops0, rank0, count0 = _top_k_rows(s0, k, pos)
    tops1, rank1, count1 = _top_k_rows(s1, k, pos)
    t0 = jnp.concatenate(tops0, axis=0)
    t1 = jnp.concatenate(tops1, axis=0)
    i8 = _row_iota(SUBLANE, tm)
    blocks = [tops0[0] + t1, tops0[1] + t1[:SUBLANE]]
    cpos = [_row_iota(k, tm), float(k) + i8]
    for a in range(2, SUBLANE):
        blocks.append(jnp.where(i8 < float(k // (a + 1)), tops0[a] + t1[:SUBLANE], NEG_INF))
        cpos.append(float(a * k) + i8)
    blocks.append(t0[SUBLANE:] + tops1[0])
    cpos.append((i8 + float(SUBLANE)) * float(k))
    cand = jnp.concatenate(blocks, axis=0)
    bests, crank, count2 = _top_k_rows(cand, k, jnp.concatenate(cpos, axis=0) if tie_safe else None)

    sel = jnp.where(crank < float(k), 1.0, 0.0)
    z = jnp.zeros_like(bests[0])
    for r in range(k):
        z = z + jnp.exp(bests[r] - bests[0])
    n0 = jnp.zeros_like(s0)
    row = 0
    for a in range(k):
        rows_a = k if a == 0 else (SUBLANE if a < SUBLANE else 1)
        n_a = jnp.sum(sel[row:row + rows_a, :], axis=0, keepdims=True)
        n0 = jnp.where(rank0 == float(a), n_a, n0)
        row += rows_a
    w0 = (0.5 * jnp.exp(s0 - tops0[0])) / z
    w1 = jnp.exp(s1 - tops1[0])
    most = jnp.max(jnp.maximum(jnp.maximum(count0, count1), count2))
    return w0, n0, w1, rank1, most


def _peer_route_kernel(s_ref, w0_ref, n0_ref, w1_ref, r1_ref):
    k = PEER_TOPK

    def emit(tie_safe):
        most = None
        for g in range(s_ref.shape[-1] // LANE):
            lanes = pl.ds(g * LANE, LANE)
            w0, n0, w1, r1, most_g = _peer_gate_factors(s_ref[0, :, lanes], s_ref[1, :, lanes],
                                                        k, tie_safe)
            w0_ref[:, lanes] = w0
            n0_ref[:, lanes] = n0
            w1_ref[:, lanes] = w1.astype(BF16)
            r1_ref[:, lanes] = r1.astype(BF16)
            most = most_g if most is None else jnp.maximum(most, most_g)
        return most

    most = emit(tie_safe=False)

    @pl.when(most > float(k))
    def _():
        emit(tie_safe=True)


def _peer_route(scores, tm):
    n_hc, n_keys, t = scores.shape
    heads = n_hc // 2
    out32 = jax.ShapeDtypeStruct((heads, n_keys, t), F32)
    out16 = jax.ShapeDtypeStruct((heads, n_keys, t), BF16)
    ospec = pl.BlockSpec((None, n_keys, tm), lambda i, h: (h, 0, i))
    return pl.pallas_call(
        _peer_route_kernel,
        grid=(t // tm, heads),
        in_specs=[pl.BlockSpec((2, n_keys, tm), lambda i, h: (h, 0, i))],
        out_specs=[ospec, ospec, ospec, ospec],
        out_shape=[out32, out32, out16, out16],
        compiler_params=_params("parallel", "parallel"),
        name="peer_route",
    )(scores)


def _peer_experts_kernel(hn0_ref, hn1_ref, down_ref, upt_prev_ref, upt_ref,
                         w0a_ref, n0a_ref, w1a_ref, r1a_ref, w0b_ref, n0b_ref, w1b_ref, r1b_ref,
                         oa_ref, ob_ref, pa_ref, pb_ref, *, heads, n_keys):
    g = pl.program_id(1)
    n_steps = pl.num_programs(1) - 1
    half = pa_ref.shape[1]

    @pl.when(g == 0)
    def _():
        oa_ref[...] = jnp.zeros_like(oa_ref)
        ob_ref[...] = jnp.zeros_like(ob_ref)
        pb_ref[...] = jnp.zeros_like(pb_ref)

    blocks = ((hn0_ref, w0a_ref, n0a_ref, w1a_ref, r1a_ref, oa_ref, pa_ref.at[0], pb_ref.at[0]),
              (hn1_ref, w0b_ref, n0b_ref, w1b_ref, r1b_ref, ob_ref, pa_ref.at[1], pb_ref.at[1]))

    @pl.when(g < n_steps)
    def _():
        _peer_token_block(*blocks[0], down_ref, upt_prev_ref, upt_ref, heads, n_keys, half, False)

    @pl.when(g <= n_steps - 1)
    def _():
        _peer_token_block(*blocks[1], down_ref, upt_prev_ref, upt_ref, heads, n_keys, half, False)

    @pl.when(g == n_steps)
    def _():
        for blk in blocks:
            _peer_token_block(*blk, down_ref, upt_prev_ref, upt_ref, heads, n_keys, half, True)


def _peer_token_block(hn_ref, w0_ref, n0_ref, w1_ref, r1_ref, acc_ref, pa_ref, pb_ref,
                      down_ref, upt_prev_ref, upt_ref, heads, n_keys, half, last):
    tb = hn_ref.shape[0]
    rows_per_chunk = half // n_keys

    def activate(c, p_ref):
        a = _nt_dot(down_ref[pl.ds(c * half, half), :], hn_ref[...])
        act = (a * (1.0 + lax.erf(a * (1.0 / math.sqrt(2.0))))).astype(BF16)
        for rr in range(rows_per_chunk):
            r = c * rows_per_chunk + rr
            for tc in range(tb // LANE):
                lanes = slice(tc * LANE, (tc + 1) * LANE)
                gate = jnp.zeros((n_keys // BF16_ROWS, BF16_ROWS, LANE), BF16)
                for h in range(heads):
                    n_row = jnp.broadcast_to(n0_ref[h, pl.ds(r, 1), lanes], (BF16_ROWS, LANE))
                    w_row = jnp.broadcast_to(w0_ref[h, pl.ds(r, 1), lanes], (BF16_ROWS, LANE))
                    n_row = n_row.astype(BF16)[None]
                    w_row = w_row.astype(BF16)[None]
                    live = jnp.maximum(n_row - r1_ref[h, :, :, lanes], 0.0)
                    gate = gate + w_row * jnp.minimum(live, w1_ref[h, :, :, lanes])
                rows = slice(rr * n_keys, (rr + 1) * n_keys)
                p_ref[rows, lanes] = gate.reshape(n_keys, LANE) * act[rows, lanes]

    def up_project(w_ref, p_ref):
        for s in range(0, acc_ref.shape[0], MXU_ACC_ROWS):
            rows = pl.ds(s, MXU_ACC_ROWS)
            acc_ref[rows, :] += jnp.dot(w_ref[rows, :], p_ref[...], preferred_element_type=F32)

    if last:
        up_project(upt_prev_ref, pb_ref)
    else:
        up_project(upt_prev_ref, pb_ref)
        activate(0, pa_ref)
        activate(1, pb_ref)
        up_project(upt_ref, pa_ref)


def _peer_experts(hn, down, up, w0, n0, w1, r1, tb, rows_per_step):
    t, d = hn.shape
    heads, n_keys, _ = w0.shape
    assert rows_per_step % 2 == 0 and t % (2 * tb) == 0
    half = rows_per_step * n_keys // 2
    n_chunks = down.shape[0] // half
    n_steps = n_chunks // 2
    upt = up.reshape(n_chunks, half, d).transpose(0, 2, 1)
    w1 = w1.reshape(heads, n_keys // BF16_ROWS, BF16_ROWS, t)
    r1 = r1.reshape(heads, n_keys // BF16_ROWS, BF16_ROWS, t)
    once = dict(pipeline_mode=pl.Buffered(1))

    def token_specs(k):
        fspec = pl.BlockSpec((heads, n_keys // BF16_ROWS, BF16_ROWS, tb),
                             lambda i, g: (0, 0, 0, 2 * i + k))
        rspec = pl.BlockSpec((heads, rows_per_step, tb),
                             lambda i, g: (0, jnp.minimum(g, n_steps - 1), 2 * i + k))
        return [rspec, rspec, fspec, fspec]

    kernel = functools.partial(_peer_experts_kernel, heads=heads, n_keys=n_keys)
    out = jax.ShapeDtypeStruct((d, t // 2), F32)
    return pl.pallas_call(
        kernel,
        grid=(t // (2 * tb), n_steps + 1),
        in_specs=[
            pl.BlockSpec((tb, d), lambda i, g: (2 * i, 0), **once),
            pl.BlockSpec((tb, d), lambda i, g: (2 * i + 1, 0), **once),
            pl.BlockSpec((2 * half, d), lambda i, g: (jnp.minimum(g, n_steps - 1), 0)),
            pl.BlockSpec((None, d, half), lambda i, g: (jnp.maximum(2 * g - 1, 0), 0, 0)),
            pl.BlockSpec((None, d, half), lambda i, g: (jnp.minimum(2 * g, n_chunks - 1), 0, 0)),
            *token_specs(0), *token_specs(1),
        ],
        out_specs=[pl.BlockSpec((d, tb), lambda i, g: (0, i)),
                   pl.BlockSpec((d, tb), lambda i, g: (0, i))],
        out_shape=[out, out],
        scratch_shapes=[
            pltpu.VMEM((2, half, tb), BF16),
            pltpu.VMEM((2, half, tb), BF16),
        ],
        compiler_params=_params("parallel", "arbitrary"),
        name="peer_experts",
    )(hn, hn, down, upt, upt, w0, n0, w1, r1, w0, n0, w1, r1)


def _residual_kernel(h_ref, pe_ref, po_ref, y_ref):
    tb = pe_ref.shape[1]
    y_ref[pl.ds(0, tb), :] = h_ref[pl.ds(0, tb), :] + pe_ref[...].T
    y_ref[pl.ds(tb, tb), :] = h_ref[pl.ds(tb, tb), :] + po_ref[...].T


def _residual(h2, peer_even, peer_odd, tb):
    t, d = h2.shape
    return pl.pallas_call(
        _residual_kernel,
        grid=(t // (2 * tb),),
        in_specs=[pl.BlockSpec((2 * tb, d), lambda i: (i, 0)),
                  pl.BlockSpec((d, tb), lambda i: (0, i)),
                  pl.BlockSpec((d, tb), lambda i: (0, i))],
        out_specs=pl.BlockSpec((2 * tb, d), lambda i: (i, 0)),
        out_shape=jax.ShapeDtypeStruct((t, d), F32),
        compiler_params=_params("parallel"),
        name="peer_residual",
    )(h2, peer_even, peer_odd)


def _largest_tile(total, limit, unit):
    best = unit
    for cand in range(unit, min(total, limit) + 1, unit):
        if total % cand == 0:
            best = cand
    return best


def _pad_cols(a, cols):
    return jnp.pad(a, ((0, 0), (0, cols - a.shape[1])))


def kernel(x, attn_norm, w_in, sb_q_norm, sb_k_norm, conv_w, conv_b, dt_bias, a_log, d_skip,
           sb_out_norm, ssd_out_norm, w_out, ffn_norm, peer_query, peer_sub_keys, peer_down, peer_up):
    b, s, d = x.shape
    t = b * s
    depth = w_in.shape[0]
    sb_width = sb_out_norm.shape[-1]
    sb_heads = sb_width // SB_HEAD_DIM
    ssd_width = ssd_out_norm.shape[-1]
    ssd_heads = dt_bias.shape[-1]
    assert ssd_heads * SSD_HEAD_DIM == ssd_width and ssd_heads <= LANE
    assert t % MXU_ACC_ROWS == 0 and d % MXU_ACC_ROWS == 0
    peer_heads, _, n_keys, half = peer_sub_keys.shape[1:]
    n_in = w_in.shape[-1]
    n_in_pad = -(-n_in // LANE) * LANE

    tm_in = _largest_tile(t, 2 * MXU_ACC_ROWS, MXU_ACC_ROWS)
    tn_in = _largest_tile(n_in_pad, 1152, LANE)
    tq = _largest_tile(s, 256, LANE)
    sb_chains = _largest_tile(s // tq, 4, 1)
    tm_out = _largest_tile(t, 256, SUBLANE)
    tm_route = _largest_tile(t, 256, LANE)
    tb_peer = _largest_tile(t, 512, LANE)
    rows_per_step = _largest_tile(n_keys, SUBLANE, SUBLANE)

    h2 = x.reshape(t, d)
    for layer in range(depth):
        w_in_l = _pad_cols(w_in[layer], n_in_pad).astype(BF16)
        proj = _in_proj(h2, attn_norm[layer][None, :], w_in_l, tm_in, tn_in).reshape(b, s, n_in_pad)

        attn = _sb_attn(proj, sb_q_norm[layer][None, :], sb_k_norm[layer][None, :], sb_heads, tq,
                        sb_chains)

        cw = conv_w[layer][:, 0, :]
        cb = conv_b[layer][None, :]
        ssd = _ssd(proj, cw[:, :ssd_width], cb[:, :ssd_width], cw[:, ssd_width:], cb[:, ssd_width:],
                   _pad_cols(dt_bias[layer][None, :], LANE), _pad_cols(a_log[layer][None, :], LANE),
                   jnp.repeat(d_skip[layer], SSD_HEAD_DIM)[None, :], ssd_out_norm[layer][None, :],
                   ssd_heads, sb_width)

        w_out_l = w_out[layer].astype(BF16)
        h2, hn, scores = _out_proj(attn.reshape(t, sb_width), ssd.reshape(t, ssd_width), h2,
                                   sb_out_norm[layer][None, :], w_out_l[:sb_width], w_out_l[sb_width:],
                                   ffn_norm[layer][None, :], peer_query[layer].astype(BF16),
                                   peer_sub_keys[layer].reshape(2 * peer_heads, n_keys, half), tm_out)

        w0, n0, w1, r1 = _peer_route(scores, tm_route)
        peer_even, peer_odd = _peer_experts(hn, peer_down[layer].astype(BF16),
                                            peer_up[layer].astype(BF16),
                                            w0, n0, w1, r1, tb_peer, rows_per_step)
        h2 = _residual(h2, peer_even, peer_odd, tb_peer)
    return h2.reshape(b, s, d)
```

```python
import functools
import math

import jax
import jax.numpy as jnp
from jax import lax
from jax.experimental import pallas as pl
from jax.experimental.pallas import tpu as pltpu

F32 = jnp.float32
BF16 = jnp.bfloat16

RMS_EPS = 1e-6
SB_HEAD_DIM = 128
SSD_HEAD_DIM = 64
SSD_GROUPS = 2
SSD_STATE = 128
SSD_CONV = 4
SSD_CHUNK = 128
PEER_TOPK = 16
LANE = 128
SUBLANE = 8
BF16_ROWS = 16
MXU_TILE = 256
MXU_ACC_ROWS = 512
VMEM_LIMIT = 56 * 1024 * 1024

EXP_ZERO_BOUND = -105.0

NEG_INF = float("-inf")


def _nt_dot(a, b, precision=None):
    return lax.dot_general(a, b, (((1,), (1,)), ((), ())), precision=precision,
                           preferred_element_type=F32)


def _params(*sem):
    return pltpu.CompilerParams(dimension_semantics=sem, vmem_limit_bytes=VMEM_LIMIT)


def _in_proj_kernel(x_ref, g_ref, w_ref, o_ref, xn_ref):
    @pl.when(pl.program_id(1) == 0)
    def _():
        x = x_ref[...]
        inv = lax.rsqrt(jnp.mean(x * x, axis=-1, keepdims=True) + RMS_EPS)
        xn_ref[...] = ((x * inv) * g_ref[...]).astype(BF16)

    for s in range(0, xn_ref.shape[0], MXU_ACC_ROWS):
        rows = pl.ds(s, MXU_ACC_ROWS)
        o_ref[rows, :] = jnp.dot(xn_ref[rows, :], w_ref[...], preferred_element_type=F32)


def _in_proj(x2, gain, w, tm, tn):
    t, d = x2.shape
    n = w.shape[1]
    return pl.pallas_call(
        _in_proj_kernel,
        grid=(t // tm, n // tn),
        in_specs=[
            pl.BlockSpec((tm, d), lambda i, j: (i, 0)),
            pl.BlockSpec((1, d), lambda i, j: (0, 0)),
            pl.BlockSpec((d, tn), lambda i, j: (0, j)),
        ],
        out_specs=pl.BlockSpec((tm, tn), lambda i, j: (i, j)),
        out_shape=jax.ShapeDtypeStruct((t, n), F32),
        scratch_shapes=[pltpu.VMEM((tm, d), BF16)],
        compiler_params=_params("parallel", "arbitrary"),
        name="in_proj",
    )(x2, gain, w)


def _head_rms(x, gain):
    inv = lax.rsqrt(jnp.mean(x * x, axis=-1, keepdims=True) + RMS_EPS)
    return (x * inv) * gain


def _sb_attn_kernel(q_ref, k_ref, v_ref, gq_ref, gk_ref, o_ref, *, tq, scale, chains):
    first = pl.program_id(2) * chains
    qbs = [(_head_rms(q_ref[pl.ds(c * tq, tq), :], gq_ref[...]) * scale).astype(BF16)
           for c in range(chains)]
    row = lax.broadcasted_iota(jnp.int32, (tq, tq), 0)
    col = lax.broadcasted_iota(jnp.int32, (tq, tq), 1)
    causal = col < row
    later_keys = jnp.where(row > col, 1.0, 0.0).astype(BF16)

    def tiles(js, carries, accs, diagonal):
        starts = [pl.multiple_of(j * tq, tq) for j in js]
        kbs = [_head_rms(k_ref[pl.ds(st, tq), :], gk_ref[...]).astype(BF16) for st in starts]
        vbs = [v_ref[pl.ds(st, tq), :].astype(BF16) for st in starts]
        ss = [_nt_dot(qb, kb) for qb, kb in zip(qbs, kbs)]
        softs = [jnp.log1p(jnp.exp(-jnp.abs(s))) for s in ss]
        log_betas = [jnp.minimum(s, 0.0) - soft for s, soft in zip(ss, softs)]
        log_keeps = [lb - s for lb, s in zip(log_betas, ss)]
        if diagonal:
            log_keeps = [jnp.where(causal, lk, 0.0) for lk in log_keeps]
        his = [lk.astype(BF16) for lk in log_keeps]
        los = [(lk - hi.astype(F32)).astype(BF16) for lk, hi in zip(log_keeps, his)]
        laters = [jnp.dot(hi, later_keys, preferred_element_type=F32)
                  + jnp.dot(lo, later_keys, preferred_element_type=F32) for hi, lo in zip(his, los)]
        ws = [jnp.exp(lb + later + carry) for lb, later, carry in zip(log_betas, laters, carries)]
        if diagonal:
            ws = [jnp.where(causal, w, 0.0) for w in ws]
        accs = [acc + jnp.dot(w.astype(BF16), vb, preferred_element_type=F32)
                for acc, w, vb in zip(accs, ws, vbs)]
        carries = [carry + jnp.sum(lk, axis=-1, keepdims=True)
                   for carry, lk in zip(carries, log_keeps)]
        return carries, accs

    carries, accs = tiles([first + c for c in range(chains)],
                          [jnp.zeros((tq, 1), F32)] * chains,
                          [jnp.zeros((tq, SB_HEAD_DIM), F32)] * chains, True)

    def cond(state):
        n, carries = state[0], state[1:1 + chains]
        go = None
        for c in range(chains):
            go_c = jnp.logical_and(first + c - n >= 0, jnp.max(carries[c]) > EXP_ZERO_BOUND)
            go = go_c if go is None else jnp.logical_or(go, go_c)
        return go

    def body(state):
        n, carries, accs = state[0], list(state[1:1 + chains]), list(state[1 + chains:])
        js = [first + c - n for c in range(chains)]
        new_carries, new_accs = tiles([jnp.maximum(j, 0) for j in js], carries, accs, False)
        for c in range(chains):
            exists = js[c] >= 0
            carries[c] = jnp.where(exists, new_carries[c], carries[c])
            accs[c] = jnp.where(exists, new_accs[c], accs[c])
        return (n + 1, *carries, *accs)

    state = lax.while_loop(cond, body, (jnp.int32(1), *carries, *accs))
    for c in range(chains):
        o_ref[pl.ds(c * tq, tq), :] = state[1 + chains + c]


def _sb_attn(proj, gq, gk, heads, tq, chains):
    b, s, _ = proj.shape
    rows = tq * chains
    kernel = functools.partial(_sb_attn_kernel, tq=tq, scale=SB_HEAD_DIM ** -0.5, chains=chains)
    return pl.pallas_call(
        kernel,
        grid=(b, heads, s // rows),
        in_specs=[
            pl.BlockSpec((None, rows, SB_HEAD_DIM), lambda bi, h, i: (bi, i, h)),
            pl.BlockSpec((None, s, SB_HEAD_DIM), lambda bi, h, i: (bi, 0, heads + h)),
            pl.BlockSpec((None, s, SB_HEAD_DIM), lambda bi, h, i: (bi, 0, 2 * heads + h)),
            pl.BlockSpec((1, SB_HEAD_DIM), lambda bi, h, i: (0, 0)),
            pl.BlockSpec((1, SB_HEAD_DIM), lambda bi, h, i: (0, 0)),
        ],
        out_specs=pl.BlockSpec((None, rows, SB_HEAD_DIM), lambda bi, h, i: (bi, i, h)),
        out_shape=jax.ShapeDtypeStruct((b, s, heads * SB_HEAD_DIM), F32),
        compiler_params=_params("parallel", "parallel", "arbitrary"),
        name="sb_attn",
    )(proj, proj, proj, gq, gk)


def _silu(x):
    return x * (1.0 / (1.0 + jnp.exp(-x)))


def _softplus(x):
    return jnp.maximum(x, 0.0) + jnp.log1p(jnp.exp(-jnp.abs(x)))


def _causal_conv(cur, ext_ref, w_ref, b_ref):
    l = cur.shape[0]
    ext_ref[pl.ds(SUBLANE, l), :] = cur
    out = b_ref[...] + w_ref[pl.ds(SSD_CONV - 1, 1), :] * cur
    for tap in range(SSD_CONV - 1):
        shift = SSD_CONV - 1 - tap
        out = out + w_ref[pl.ds(tap, 1), :] * ext_ref[pl.ds(SUBLANE - shift, l), :]
    ext_ref[pl.ds(0, SUBLANE), :] = cur[l - SUBLANE:, :]
    return out


def _ssd_kernel(xs_ref, bc_ref, dt_ref, z_ref, cwx_ref, cbx_ref, cwbc_ref, cbbc_ref,
                dtb_ref, alog_ref, dskip_ref, gain_ref, o_ref,
                extx_ref, extbc_ref, state_ref, *, heads):
    l = SSD_CHUNK
    n = SSD_STATE
    p = SSD_HEAD_DIM
    width = heads * p
    hg = heads // SSD_GROUPS
    gw = hg * p

    @pl.when(pl.program_id(1) == 0)
    def _():
        extx_ref[...] = jnp.zeros_like(extx_ref)
        extbc_ref[...] = jnp.zeros_like(extbc_ref)
        state_ref[...] = jnp.zeros_like(state_ref)

    xs = _silu(_causal_conv(xs_ref[...], extx_ref, cwx_ref, cbx_ref))
    bc = _silu(_causal_conv(bc_ref[...], extbc_ref, cwbc_ref, cbbc_ref))

    dt = _softplus(dt_ref[...] + dtb_ref[...])
    a_dt = dt * (-jnp.exp(alog_ref[...]))
    r = lax.broadcasted_iota(jnp.int32, (l, l), 0)
    c = lax.broadcasted_iota(jnp.int32, (l, l), 1)
    tril = r >= c
    a_cs = jnp.dot(jnp.where(tril, 1.0, 0.0), a_dt, precision=lax.Precision.HIGHEST,
                   preferred_element_type=F32)
    a_cs_t = a_cs.T

    er = lax.broadcasted_iota(jnp.int32, (LANE, width), 0)
    ec = lax.broadcasted_iota(jnp.int32, (LANE, width), 1)
    expand = jnp.where(ec // p == er, 1.0, 0.0)
    dt_e = jnp.dot(dt, expand, precision=lax.Precision.HIGHEST, preferred_element_type=F32)
    a_cs_e = jnp.dot(a_cs, expand, precision=lax.Precision.HIGHEST, preferred_element_type=F32)
    a_end_e = a_cs_e[l - 1:l, :]

    xf = xs * dt_e
    xd = (xf * jnp.exp(a_end_e - a_cs_e)).astype(BF16)
    xfb = xf.astype(BF16)
    decay_out = jnp.exp(a_cs_e)
    chunk_decay = jnp.exp(a_end_e)

    y_parts = []
    for g in range(SSD_GROUPS):
        bm = bc[:, g * n:(g + 1) * n]
        cm = bc[:, (SSD_GROUPS + g) * n:(SSD_GROUPS + g + 1) * n]
        bmb = bm.astype(BF16)
        cmb = cm.astype(BF16)
        cb = _nt_dot(cmb, bmb)
        cols = slice(g * gw, (g + 1) * gw)
        prev = state_ref[:, cols]
        y_off = jnp.dot(cmb, prev.astype(BF16), preferred_element_type=F32) * decay_out[:, cols]
        new_states = jnp.dot(bm.T.astype(BF16), xd[:, cols], preferred_element_type=F32)
        state_ref[:, cols] = prev * chunk_decay[:, cols] + new_states
        for hh in range(hg):
            h = g * hg + hh
            seg = a_cs[:, h:h + 1] - a_cs_t[h:h + 1, :]
            decay = jnp.where(tril, jnp.exp(jnp.where(tril, seg, 0.0)), 0.0)
            scores = (cb * decay).astype(BF16)
            y_diag = jnp.dot(scores, xfb[:, h * p:(h + 1) * p], preferred_element_type=F32)
            y_parts.append(y_diag + y_off[:, hh * p:(hh + 1) * p])
    y = jnp.concatenate(y_parts, axis=-1) + dskip_ref[...] * xs

    y = y * _silu(z_ref[...])
    outs = []
    for g in range(SSD_GROUPS):
        yg = y[:, g * gw:(g + 1) * gw]
        inv = lax.rsqrt(jnp.mean(yg * yg, axis=-1, keepdims=True) + RMS_EPS)
        outs.append(yg * inv)
    o_ref[...] = jnp.concatenate(outs, axis=-1) * gain_ref[...]


def _ssd(proj, cwx, cbx, cwbc, cbbc, dtb, alog, dskip_e, gain, heads, sb_width):
    b, s, _ = proj.shape
    l = SSD_CHUNK
    width = heads * SSD_HEAD_DIM
    bcw = 2 * SSD_GROUPS * SSD_STATE
    z_off = 3 * sb_width
    xs_off = z_off + width
    bc_off = xs_off + width
    dt_off = bc_off + bcw
    assert z_off % width == 0 and xs_off % width == 0 and bc_off % bcw == 0 and dt_off % LANE == 0
    const = lambda shape: pl.BlockSpec(shape, lambda bi, ci: (0, 0))
    kernel = functools.partial(_ssd_kernel, heads=heads)
    return pl.pallas_call(
        kernel,
        grid=(b, s // l),
        in_specs=[
            pl.BlockSpec((None, l, width), lambda bi, ci: (bi, ci, xs_off // width)),
            pl.BlockSpec((None, l, bcw), lambda bi, ci: (bi, ci, bc_off // bcw)),
            pl.BlockSpec((None, l, LANE), lambda bi, ci: (bi, ci, dt_off // LANE)),
            pl.BlockSpec((None, l, width), lambda bi, ci: (bi, ci, z_off // width)),
            const((SSD_CONV, width)), const((1, width)),
            const((SSD_CONV, bcw)), const((1, bcw)),
            const((1, LANE)), const((1, LANE)), const((1, width)), const((1, width)),
        ],
        out_specs=pl.BlockSpec((None, l, width), lambda bi, ci: (bi, ci, 0)),
        out_shape=jax.ShapeDtypeStruct((b, s, width), F32),
        scratch_shapes=[
            pltpu.VMEM((SUBLANE + l, width), F32),
            pltpu.VMEM((SUBLANE + l, bcw), F32),
            pltpu.VMEM((SSD_STATE, width), F32),
        ],
        compiler_params=_params("parallel", "arbitrary"),
        name="ssd",
    )(proj, proj, proj, proj, cwx, cbx, cwbc, cbbc, dtb, alog, dskip_e, gain)


def _out_proj_kernel(attn_ref, ssd_ref, x_ref, ga_ref, wa_ref, ws_ref, gf_ref, wq_ref, keys_ref,
                     h_ref, hn_ref, s_ref):
    a = attn_ref[...]
    inv = lax.rsqrt(jnp.mean(a * a, axis=-1, keepdims=True) + RMS_EPS)
    an = ((a * inv) * ga_ref[...]).astype(BF16)
    h = (x_ref[...]
         + jnp.dot(an, wa_ref[...], preferred_element_type=F32)
         + jnp.dot(ssd_ref[...].astype(BF16), ws_ref[...], preferred_element_type=F32))
    h_ref[...] = h
    inv_h = lax.rsqrt(jnp.mean(h * h, axis=-1, keepdims=True) + RMS_EPS)
    hn = ((h * inv_h) * gf_ref[...]).astype(BF16)
    hn_ref[...] = hn
    q = jnp.dot(hn, wq_ref[...], preferred_element_type=F32)
    half = keys_ref.shape[-1]
    for hc in range(keys_ref.shape[0]):
        s_ref[hc] = _nt_dot(keys_ref[hc], q[:, hc * half:(hc + 1) * half],
                            precision=lax.Precision.HIGHEST)


def _out_proj(attn2, ssd2, x2, ga, wa, ws, gf, wq, keys, tm):
    t, d = x2.shape
    wa_rows, ws_rows = wa.shape[0], ws.shape[0]
    n_hc, n_keys, half = keys.shape
    once = dict(pipeline_mode=pl.Buffered(1))
    return pl.pallas_call(
        _out_proj_kernel,
        grid=(t // tm,),
        in_specs=[
            pl.BlockSpec((tm, wa_rows), lambda i: (i, 0)),
            pl.BlockSpec((tm, ws_rows), lambda i: (i, 0)),
            pl.BlockSpec((tm, d), lambda i: (i, 0)),
            pl.BlockSpec((1, wa_rows), lambda i: (0, 0)),
            pl.BlockSpec((wa_rows, d), lambda i: (0, 0), **once),
            pl.BlockSpec((ws_rows, d), lambda i: (0, 0), **once),
            pl.BlockSpec((1, d), lambda i: (0, 0)),
            pl.BlockSpec((d, n_hc * half), lambda i: (0, 0), **once),
            pl.BlockSpec((n_hc, n_keys, half), lambda i: (0, 0, 0), **once),
        ],
        out_specs=[pl.BlockSpec((tm, d), lambda i: (i, 0)),
                   pl.BlockSpec((tm, d), lambda i: (i, 0)),
                   pl.BlockSpec((n_hc, n_keys, tm), lambda i: (0, 0, i))],
        out_shape=[jax.ShapeDtypeStruct((t, d), F32), jax.ShapeDtypeStruct((t, d), BF16),
                   jax.ShapeDtypeStruct((n_hc, n_keys, t), F32)],
        compiler_params=_params("parallel"),
        name="out_proj",
    )(attn2, ssd2, x2, ga, wa, ws, gf, wq, keys)


def _top_k_rows(scores, k, pos=None):
    work = scores
    rank = jnp.full(scores.shape, float(k), F32)
    tops = []
    for r in range(k):
        m = jnp.max(work, axis=0, keepdims=True)
        hit = work == m
        if pos is not None:
            first = jnp.min(jnp.where(hit, pos, float("inf")), axis=0, keepdims=True)
            hit = pos == first
        rank = jnp.where(hit, float(r), rank)
        work = jnp.where(hit, NEG_INF, work)
        tops.append(m)
    count = jnp.sum(jnp.where(rank < float(k), 1.0, 0.0), axis=0, keepdims=True)
    return tops, rank, count


def _row_iota(rows, cols):
    return lax.broadcasted_iota(jnp.int32, (rows, cols), 0).astype(F32)


def _peer_gate_factors(s0, s1, k, tie_safe):
    assert k == 2 * SUBLANE
    keys, tm = s0.shape
    pos = _row_iota(keys, tm) if tie_safe else None
    tops0, rank0, count0 = _top_k_rows(s0, k, pos)
    tops1, rank1, count1 = _top_k_rows(s1, k, pos)
    t0 = jnp.concatenate(tops0, axis=0)
    t1 = jnp.concatenate(tops1, axis=0)
    i8 = _row_iota(SUBLANE, tm)
    blocks = [tops0[0] + t1, tops0[1] + t1[:SUBLANE]]
    cpos = [_row_iota(k, tm), float(k) + i8]
    for a in range(2, SUBLANE):
        blocks.append(jnp.where(i8 < float(k // (a + 1)), tops0[a] + t1[:SUBLANE], NEG_INF))
        cpos.append(float(a * k) + i8)
    blocks.append(t0[SUBLANE:] + tops1[0])
    cpos.append((i8 + float(SUBLANE)) * float(k))
    cand = jnp.concatenate(blocks, axis=0)
    bests, crank, count2 = _top_k_rows(cand, k, jnp.concatenate(cpos, axis=0) if tie_safe else None)

    sel = jnp.where(crank < float(k), 1.0, 0.0)
    z = jnp.zeros_like(bests[0])
    for r in range(k):
        z = z + jnp.exp(bests[r] - bests[0])
    n0 = jnp.zeros_like(s0)
    row = 0
    for a in range(SUBLANE):
        rows_a = k if a == 0 else SUBLANE
        n_a = jnp.sum(sel[row:row + rows_a, :], axis=0, keepdims=True)
        n0 = jnp.where(rank0 == float(a), n_a, n0)
        row += rows_a
    tail = jnp.sum(sel[row:row + SUBLANE, :], axis=0, keepdims=True)
    prefix = jnp.where(rank0 < float(SUBLANE) + tail, 1.0, 0.0)
    n0 = jnp.where(rank0 >= float(SUBLANE), prefix, n0)
    w0 = (0.5 * jnp.exp(s0 - tops0[0])) / z
    w1 = jnp.exp(s1 - tops1[0])
    most = jnp.max(jnp.maximum(jnp.maximum(count0, count1), count2))
    return w0, n0, w1, rank1, most


def _peer_route_kernel(s_ref, w0_ref, n0_ref, w1_ref, r1_ref):
    k = PEER_TOPK

    def emit(tie_safe):
        most = None
        for g in range(s_ref.shape[-1] // LANE):
            lanes = pl.ds(g * LANE, LANE)
            w0, n0, w1, r1, most_g = _peer_gate_factors(s_ref[0, :, lanes], s_ref[1, :, lanes],
                                                        k, tie_safe)
            w0_ref[:, lanes] = w0
            n0_ref[:, lanes] = n0
            w1_ref[:, lanes] = w1.astype(BF16)
            r1_ref[:, lanes] = r1.astype(BF16)
            most = most_g if most is None else jnp.maximum(most, most_g)
        return most

    most = emit(tie_safe=False)

    @pl.when(most > float(k))
    def _():
        emit(tie_safe=True)


def _peer_route(scores, tm):
    n_hc, n_keys, t = scores.shape
    heads = n_hc // 2
    out32 = jax.ShapeDtypeStruct((heads, n_keys, t), F32)
    out16 = jax.ShapeDtypeStruct((heads, n_keys, t), BF16)
    ospec = pl.BlockSpec((None, n_keys, tm), lambda i, h: (h, 0, i))
    return pl.pallas_call(
        _peer_route_kernel,
        grid=(t // tm, heads),
        in_specs=[pl.BlockSpec((2, n_keys, tm), lambda i, h: (h, 0, i))],
        out_specs=[ospec, ospec, ospec, ospec],
        out_shape=[out32, out32, out16, out16],
        compiler_params=_params("parallel", "parallel"),
        name="peer_route",
    )(scores)


def _peer_experts_kernel(hn0_ref, hn1_ref, down_ref, upt_prev_ref, upt_ref,
                         w0a_ref, n0a_ref, w1a_ref, r1a_ref, w0b_ref, n0b_ref, w1b_ref, r1b_ref,
                         oa_ref, ob_ref, pa_ref, pb_ref, *, heads, n_keys):
    g = pl.program_id(1)
    n_steps = pl.num_programs(1) - 1
    half = pa_ref.shape[1]

    @pl.when(g == 0)
    def _():
        oa_ref[...] = jnp.zeros_like(oa_ref)
        ob_ref[...] = jnp.zeros_like(ob_ref)
        pb_ref[...] = jnp.zeros_like(pb_ref)

    blocks = ((hn0_ref, w0a_ref, n0a_ref, w1a_ref, r1a_ref, oa_ref, pa_ref.at[0], pb_ref.at[0]),
              (hn1_ref, w0b_ref, n0b_ref, w1b_ref, r1b_ref, ob_ref, pa_ref.at[1], pb_ref.at[1]))

    @pl.when(g < n_steps)
    def _():
        _peer_token_block(*blocks[0], down_ref, upt_prev_ref, upt_ref, heads, n_keys, half, False)

    @pl.when(g <= n_steps - 1)
    def _():
        _peer_token_block(*blocks[1], down_ref, upt_prev_ref, upt_ref, heads, n_keys, half, False)

    @pl.when(g == n_steps)
    def _():
        for blk in blocks:
            _peer_token_block(*blk, down_ref, upt_prev_ref, upt_ref, heads, n_keys, half, True)


def _peer_token_block(hn_ref, w0_ref, n0_ref, w1_ref, r1_ref, acc_ref, pa_ref, pb_ref,
                      down_ref, upt_prev_ref, upt_ref, heads, n_keys, half, last):
    tb = hn_ref.shape[0]
    rows_per_chunk = half // n_keys

    def activate(c, p_ref):
        a = _nt_dot(down_ref[pl.ds(c * half, half), :], hn_ref[...])
        act = (a * (1.0 + lax.erf(a * (1.0 / math.sqrt(2.0))))).astype(BF16)
        for rr in range(rows_per_chunk):
            r = c * rows_per_chunk + rr
            for tc in range(tb // LANE):
                lanes = slice(tc * LANE, (tc + 1) * LANE)
                gate = jnp.zeros((n_keys // BF16_ROWS, BF16_ROWS, LANE), BF16)
                for h in range(heads):
                    n_row = jnp.broadcast_to(n0_ref[h, pl.ds(r, 1), lanes], (BF16_ROWS, LANE))
                    w_row = jnp.broadcast_to(w0_ref[h, pl.ds(r, 1), lanes], (BF16_ROWS, LANE))
                    n_row = n_row.astype(BF16)[None]
                    w_row = w_row.astype(BF16)[None]
                    live = jnp.maximum(n_row - r1_ref[h, :, :, lanes], 0.0)
                    gate = gate + w_row * jnp.minimum(live, w1_ref[h, :, :, lanes])
                rows = slice(rr * n_keys, (rr + 1) * n_keys)
                p_ref[rows, lanes] = gate.reshape(n_keys, LANE) * act[rows, lanes]

    def up_project(w_ref, p_ref):
        for s in range(0, acc_ref.shape[0], MXU_ACC_ROWS):
            rows = pl.ds(s, MXU_ACC_ROWS)
            acc_ref[rows, :] += jnp.dot(w_ref[rows, :], p_ref[...], preferred_element_type=F32)

    if last:
        up_project(upt_prev_ref, pb_ref)
    else:
        up_project(upt_prev_ref, pb_ref)
        activate(0, pa_ref)
        activate(1, pb_ref)
        up_project(upt_ref, pa_ref)


def _peer_experts(hn, down, up, w0, n0, w1, r1, tb, rows_per_step):
    t, d = hn.shape
    heads, n_keys, _ = w0.shape
    assert rows_per_step % 2 == 0 and t % (2 * tb) == 0
    half = rows_per_step * n_keys // 2
    n_chunks = down.shape[0] // half
    n_steps = n_chunks // 2
    upt = up.reshape(n_chunks, half, d).transpose(0, 2, 1)
    w1 = w1.reshape(heads, n_keys // BF16_ROWS, BF16_ROWS, t)
    r1 = r1.reshape(heads, n_keys // BF16_ROWS, BF16_ROWS, t)
    once = dict(pipeline_mode=pl.Buffered(1))

    def token_specs(k):
        fspec = pl.BlockSpec((heads, n_keys // BF16_ROWS, BF16_ROWS, tb),
                             lambda i, g: (0, 0, 0, 2 * i + k))
        rspec = pl.BlockSpec((heads, rows_per_step, tb),
                             lambda i, g: (0, jnp.minimum(g, n_steps - 1), 2 * i + k))
        return [rspec, rspec, fspec, fspec]

    kernel = functools.partial(_peer_experts_kernel, heads=heads, n_keys=n_keys)
    out = jax.ShapeDtypeStruct((d, t // 2), F32)
    return pl.pallas_call(
        kernel,
        grid=(t // (2 * tb), n_steps + 1),
        in_specs=[
            pl.BlockSpec((tb, d), lambda i, g: (2 * i, 0), **once),
            pl.BlockSpec((tb, d), lambda i, g: (2 * i + 1, 0), **once),
            pl.BlockSpec((2 * half, d), lambda i, g: (jnp.minimum(g, n_steps - 1), 0)),
            pl.BlockSpec((None, d, half), lambda i, g: (jnp.maximum(2 * g - 1, 0), 0, 0)),
            pl.BlockSpec((None, d, half), lambda i, g: (jnp.minimum(2 * g, n_chunks - 1), 0, 0)),
            *token_specs(0), *token_specs(1),
        ],
        out_specs=[pl.BlockSpec((d, tb), lambda i, g: (0, i)),
                   pl.BlockSpec((d, tb), lambda i, g: (0, i))],
        out_shape=[out, out],
        scratch_shapes=[
            pltpu.VMEM((2, half, tb), BF16),
            pltpu.VMEM((2, half, tb), BF16),
        ],
        compiler_params=_params("parallel", "arbitrary"),
        name="peer_experts",
    )(hn, hn, down, upt, upt, w0, n0, w1, r1, w0, n0, w1, r1)


def _residual_kernel(h_ref, pe_ref, po_ref, y_ref):
    tb = pe_ref.shape[1]
    y_ref[pl.ds(0, tb), :] = h_ref[pl.ds(0, tb), :] + pe_ref[...].T
    y_ref[pl.ds(tb, tb), :] = h_ref[pl.ds(tb, tb), :] + po_ref[...].T


def _residual(h2, peer_even, peer_odd, tb):
    t, d = h2.shape
    return pl.pallas_call(
        _residual_kernel,
        grid=(t // (2 * tb),),
        in_specs=[pl.BlockSpec((2 * tb, d), lambda i: (i, 0)),
                  pl.BlockSpec((d, tb), lambda i: (0, i)),
                  pl.BlockSpec((d, tb), lambda i: (0, i))],
        out_specs=pl.BlockSpec((2 * tb, d), lambda i: (i, 0)),
        out_shape=jax.ShapeDtypeStruct((t, d), F32),
        compiler_params=_params("parallel"),
        name="peer_residual",
    )(h2, peer_even, peer_odd)


def _largest_tile(total, limit, unit):
    best = unit
    for cand in range(unit, min(total, limit) + 1, unit):
        if total % cand == 0:
            best = cand
    return best


def _pad_cols(a, cols):
    return jnp.pad(a, ((0, 0), (0, cols - a.shape[1])))


def kernel(x, attn_norm, w_in, sb_q_norm, sb_k_norm, conv_w, conv_b, dt_bias, a_log, d_skip,
           sb_out_norm, ssd_out_norm, w_out, ffn_norm, peer_query, peer_sub_keys, peer_down, peer_up):
    b, s, d = x.shape
    t = b * s
    depth = w_in.shape[0]
    sb_width = sb_out_norm.shape[-1]
    sb_heads = sb_width // SB_HEAD_DIM
    ssd_width = ssd_out_norm.shape[-1]
    ssd_heads = dt_bias.shape[-1]
    assert ssd_heads * SSD_HEAD_DIM == ssd_width and ssd_heads <= LANE
    assert t % MXU_ACC_ROWS == 0 and d % MXU_ACC_ROWS == 0
    peer_heads, _, n_keys, half = peer_sub_keys.shape[1:]
    n_in = w_in.shape[-1]
    n_in_pad = -(-n_in // LANE) * LANE

    tm_in = _largest_tile(t, 2 * MXU_ACC_ROWS, MXU_ACC_ROWS)
    tn_in = _largest_tile(n_in_pad, 1152, LANE)
    tq = _largest_tile(s, 128, LANE)
    sb_chains = _largest_tile(s // tq, 8, 1)
    tm_out = _largest_tile(t, 256, SUBLANE)
    tm_route = _largest_tile(t, 256, LANE)
    tb_peer = _largest_tile(t, 512, LANE)
    rows_per_step = _largest_tile(n_keys, SUBLANE, SUBLANE)

    h2 = x.reshape(t, d)
    for layer in range(depth):
        w_in_l = _pad_cols(w_in[layer], n_in_pad).astype(BF16)
        proj = _in_proj(h2, attn_norm[layer][None, :], w_in_l, tm_in, tn_in).reshape(b, s, n_in_pad)

        attn = _sb_attn(proj, sb_q_norm[layer][None, :], sb_k_norm[layer][None, :], sb_heads, tq,
                        sb_chains)

        cw = conv_w[layer][:, 0, :]
        cb = conv_b[layer][None, :]
        ssd = _ssd(proj, cw[:, :ssd_width], cb[:, :ssd_width], cw[:, ssd_width:], cb[:, ssd_width:],
                   _pad_cols(dt_bias[layer][None, :], LANE), _pad_cols(a_log[layer][None, :], LANE),
                   jnp.repeat(d_skip[layer], SSD_HEAD_DIM)[None, :], ssd_out_norm[layer][None, :],
                   ssd_heads, sb_width)

        w_out_l = w_out[layer].astype(BF16)
        h2, hn, scores = _out_proj(attn.reshape(t, sb_width), ssd.reshape(t, ssd_width), h2,
                                   sb_out_norm[layer][None, :], w_out_l[:sb_width], w_out_l[sb_width:],
                                   ffn_norm[layer][None, :], peer_query[layer].astype(BF16),
                                   peer_sub_keys[layer].reshape(2 * peer_heads, n_keys, half), tm_out)

        w0, n0, w1, r1 = _peer_route(scores, tm_route)
        peer_even, peer_odd = _peer_experts(hn, peer_down[layer].astype(BF16),
                                            peer_up[layer].astype(BF16),
                                            w0, n0, w1, r1, tb_peer, rows_per_step)
        h2 = _residual(h2, peer_even, peer_odd, tb_peer)
    return h2.reshape(b, s, d)
```

```python
import functools
import math

import jax
import jax.numpy as jnp
from jax import lax
from jax.experimental import pallas as pl
from jax.experimental.pallas import tpu as pltpu

F32 = jnp.float32
BF16 = jnp.bfloat16

RMS_EPS = 1e-6
SB_HEAD_DIM = 128
SSD_HEAD_DIM = 64
SSD_GROUPS = 2
SSD_STATE = 128
SSD_CONV = 4
SSD_CHUNK = 128
PEER_TOPK = 16
LANE = 128
SUBLANE = 8
BF16_ROWS = 16
MXU_TILE = 256
MXU_ACC_ROWS = 512
VMEM_LIMIT = 56 * 1024 * 1024

EXP_ZERO_BOUND = -105.0

NEG_INF = float("-inf")


def _nt_dot(a, b, precision=None):
    return lax.dot_general(a, b, (((1,), (1,)), ((), ())), precision=precision,
                           preferred_element_type=F32)


def _params(*sem):
    return pltpu.CompilerParams(dimension_semantics=sem, vmem_limit_bytes=VMEM_LIMIT)


def _in_proj_kernel(x_ref, g_ref, w_ref, o_ref, xn_ref):
    @pl.when(pl.program_id(1) == 0)
    def _():
        x = x_ref[...]
        inv = lax.rsqrt(jnp.mean(x * x, axis=-1, keepdims=True) + RMS_EPS)
        xn_ref[...] = ((x * inv) * g_ref[...]).astype(BF16)

    for s in range(0, xn_ref.shape[0], MXU_ACC_ROWS):
        rows = pl.ds(s, MXU_ACC_ROWS)
        o_ref[rows, :] = jnp.dot(xn_ref[rows, :], w_ref[...], preferred_element_type=F32)


def _in_proj(x2, gain, w, tm, tn):
    t, d = x2.shape
    n = w.shape[1]
    return pl.pallas_call(
        _in_proj_kernel,
        grid=(t // tm, n // tn),
        in_specs=[
            pl.BlockSpec((tm, d), lambda i, j: (i, 0)),
            pl.BlockSpec((1, d), lambda i, j: (0, 0)),
            pl.BlockSpec((d, tn), lambda i, j: (0, j)),
        ],
        out_specs=pl.BlockSpec((tm, tn), lambda i, j: (i, j)),
        out_shape=jax.ShapeDtypeStruct((t, n), F32),
        scratch_shapes=[pltpu.VMEM((tm, d), BF16)],
        compiler_params=_params("parallel", "arbitrary"),
        name="in_proj",
    )(x2, gain, w)


def _head_rms(x, gain):
    inv = lax.rsqrt(jnp.mean(x * x, axis=-1, keepdims=True) + RMS_EPS)
    return (x * inv) * gain


def _sb_attn_kernel(q_ref, k_ref, v_ref, gq_ref, gk_ref, o_ref, *, tq, scale, chains):
    first = pl.program_id(2) * chains
    qbs = [(_head_rms(q_ref[pl.ds(c * tq, tq), :], gq_ref[...]) * scale).astype(BF16)
           for c in range(chains)]
    row = lax.broadcasted_iota(jnp.int32, (tq, tq), 0)
    col = lax.broadcasted_iota(jnp.int32, (tq, tq), 1)
    causal = col < row
    later_keys = jnp.where(row > col, 1.0, 0.0).astype(BF16)

    def tiles(js, carries, accs, diagonal):
        starts = [pl.multiple_of(j * tq, tq) for j in js]
        kbs = [_head_rms(k_ref[pl.ds(st, tq), :], gk_ref[...]).astype(BF16) for st in starts]
        vbs = [v_ref[pl.ds(st, tq), :].astype(BF16) for st in starts]
        ss = [_nt_dot(qb, kb) for qb, kb in zip(qbs, kbs)]
        softs = [jnp.log1p(jnp.exp(-jnp.abs(s))) for s in ss]
        log_betas = [jnp.minimum(s, 0.0) - soft for s, soft in zip(ss, softs)]
        log_keeps = [lb - s for lb, s in zip(log_betas, ss)]
        if diagonal:
            log_keeps = [jnp.where(causal, lk, 0.0) for lk in log_keeps]
        his = [lk.astype(BF16) for lk in log_keeps]
        los = [(lk - hi.astype(F32)).astype(BF16) for lk, hi in zip(log_keeps, his)]
        laters = [jnp.dot(hi, later_keys, preferred_element_type=F32)
                  + jnp.dot(lo, later_keys, preferred_element_type=F32) for hi, lo in zip(his, los)]
        ws = [jnp.exp(lb + later + carry) for lb, later, carry in zip(log_betas, laters, carries)]
        if diagonal:
            ws = [jnp.where(causal, w, 0.0) for w in ws]
        accs = [acc + jnp.dot(w.astype(BF16), vb, preferred_element_type=F32)
                for acc, w, vb in zip(accs, ws, vbs)]
        carries = [carry + jnp.sum(lk, axis=-1, keepdims=True)
                   for carry, lk in zip(carries, log_keeps)]
        return carries, accs

    carries, accs = tiles([first + c for c in range(chains)],
                          [jnp.zeros((tq, 1), F32)] * chains,
                          [jnp.zeros((tq, SB_HEAD_DIM), F32)] * chains, True)

    def cond(state):
        n, carries = state[0], state[1:1 + chains]
        go = None
        for c in range(chains):
            go_c = jnp.logical_and(first + c - n >= 0, jnp.max(carries[c]) > EXP_ZERO_BOUND)
            go = go_c if go is None else jnp.logical_or(go, go_c)
        return go

    def body(state):
        n, carries, accs = state[0], list(state[1:1 + chains]), list(state[1 + chains:])
        js = [first + c - n for c in range(chains)]
        new_carries, new_accs = tiles([jnp.maximum(j, 0) for j in js], carries, accs, False)
        for c in range(chains):
            exists = js[c] >= 0
            carries[c] = jnp.where(exists, new_carries[c], carries[c])
            accs[c] = jnp.where(exists, new_accs[c], accs[c])
        return (n + 1, *carries, *accs)

    state = lax.while_loop(cond, body, (jnp.int32(1), *carries, *accs))
    for c in range(chains):
        o_ref[pl.ds(c * tq, tq), :] = state[1 + chains + c]


def _sb_attn(proj, gq, gk, heads, tq, chains):
    b, s, _ = proj.shape
    rows = tq * chains
    kernel = functools.partial(_sb_attn_kernel, tq=tq, scale=SB_HEAD_DIM ** -0.5, chains=chains)
    return pl.pallas_call(
        kernel,
        grid=(b, heads, s // rows),
        in_specs=[
            pl.BlockSpec((None, rows, SB_HEAD_DIM), lambda bi, h, i: (bi, i, h)),
            pl.BlockSpec((None, s, SB_HEAD_DIM), lambda bi, h, i: (bi, 0, heads + h)),
            pl.BlockSpec((None, s, SB_HEAD_DIM), lambda bi, h, i: (bi, 0, 2 * heads + h)),
            pl.BlockSpec((1, SB_HEAD_DIM), lambda bi, h, i: (0, 0)),
            pl.BlockSpec((1, SB_HEAD_DIM), lambda bi, h, i: (0, 0)),
        ],
        out_specs=pl.BlockSpec((None, rows, SB_HEAD_DIM), lambda bi, h, i: (bi, i, h)),
        out_shape=jax.ShapeDtypeStruct((b, s, heads * SB_HEAD_DIM), F32),
        compiler_params=_params("parallel", "parallel", "arbitrary"),
        name="sb_attn",
    )(proj, proj, proj, gq, gk)


def _silu(x):
    return x * (1.0 / (1.0 + jnp.exp(-x)))


def _softplus(x):
    return jnp.maximum(x, 0.0) + jnp.log1p(jnp.exp(-jnp.abs(x)))


def _causal_conv(cur, ext_ref, w_ref, b_ref):
    l = cur.shape[0]
    ext_ref[pl.ds(SUBLANE, l), :] = cur
    out = b_ref[...] + w_ref[pl.ds(SSD_CONV - 1, 1), :] * cur
    for tap in range(SSD_CONV - 1):
        shift = SSD_CONV - 1 - tap
        out = out + w_ref[pl.ds(tap, 1), :] * ext_ref[pl.ds(SUBLANE - shift, l), :]
    ext_ref[pl.ds(0, SUBLANE), :] = cur[l - SUBLANE:, :]
    return out


def _ssd_kernel(xs_ref, bc_ref, dt_ref, z_ref, cwx_ref, cbx_ref, cwbc_ref, cbbc_ref,
                dtb_ref, alog_ref, dskip_ref, gain_ref, o_ref,
                extx_ref, extbc_ref, state_ref, *, heads):
    l = SSD_CHUNK
    n = SSD_STATE
    p = SSD_HEAD_DIM
    width = heads * p
    hg = heads // SSD_GROUPS
    gw = hg * p

    @pl.when(pl.program_id(1) == 0)
    def _():
        extx_ref[...] = jnp.zeros_like(extx_ref)
        extbc_ref[...] = jnp.zeros_like(extbc_ref)
        state_ref[...] = jnp.zeros_like(state_ref)

    xs = _silu(_causal_conv(xs_ref[...], extx_ref, cwx_ref, cbx_ref))
    bc = _silu(_causal_conv(bc_ref[...], extbc_ref, cwbc_ref, cbbc_ref))

    dt = _softplus(dt_ref[...] + dtb_ref[...])
    a_dt = dt * (-jnp.exp(alog_ref[...]))
    r = lax.broadcasted_iota(jnp.int32, (l, l), 0)
    c = lax.broadcasted_iota(jnp.int32, (l, l), 1)
    tril = r >= c
    a_cs = jnp.dot(jnp.where(tril, 1.0, 0.0), a_dt, precision=lax.Precision.HIGHEST,
                   preferred_element_type=F32)
    a_cs_t = a_cs.T

    er = lax.broadcasted_iota(jnp.int32, (LANE, width), 0)
    ec = lax.broadcasted_iota(jnp.int32, (LANE, width), 1)
    expand = jnp.where(ec // p == er, 1.0, 0.0)
    dt_e = jnp.dot(dt, expand, precision=lax.Precision.HIGHEST, preferred_element_type=F32)
    a_cs_e = jnp.dot(a_cs, expand, precision=lax.Precision.HIGHEST, preferred_element_type=F32)
    a_end_e = a_cs_e[l - 1:l, :]

    xf = xs * dt_e
    xd = (xf * jnp.exp(a_end_e - a_cs_e)).astype(BF16)
    xfb = xf.astype(BF16)
    decay_out = jnp.exp(a_cs_e)
    chunk_decay = jnp.exp(a_end_e)

    y_parts = []
    for g in range(SSD_GROUPS):
        bm = bc[:, g * n:(g + 1) * n]
        cm = bc[:, (SSD_GROUPS + g) * n:(SSD_GROUPS + g + 1) * n]
        bmb = bm.astype(BF16)
        cmb = cm.astype(BF16)
        cb = _nt_dot(cmb, bmb)
        cols = slice(g * gw, (g + 1) * gw)
        prev = state_ref[:, cols]
        y_off = jnp.dot(cmb, prev.astype(BF16), preferred_element_type=F32) * decay_out[:, cols]
        new_states = jnp.dot(bm.T.astype(BF16), xd[:, cols], preferred_element_type=F32)
        state_ref[:, cols] = prev * chunk_decay[:, cols] + new_states
        for hh in range(hg):
            h = g * hg + hh
            seg = a_cs[:, h:h + 1] - a_cs_t[h:h + 1, :]
            decay = jnp.where(tril, jnp.exp(jnp.where(tril, seg, 0.0)), 0.0)
            scores = (cb * decay).astype(BF16)
            y_diag = jnp.dot(scores, xfb[:, h * p:(h + 1) * p], preferred_element_type=F32)
            y_parts.append(y_diag + y_off[:, hh * p:(hh + 1) * p])
    y = jnp.concatenate(y_parts, axis=-1) + dskip_ref[...] * xs

    y = y * _silu(z_ref[...])
    outs = []
    for g in range(SSD_GROUPS):
        yg = y[:, g * gw:(g + 1) * gw]
        inv = lax.rsqrt(jnp.mean(yg * yg, axis=-1, keepdims=True) + RMS_EPS)
        outs.append(yg * inv)
    o_ref[...] = jnp.concatenate(outs, axis=-1) * gain_ref[...]


def _ssd(proj, cwx, cbx, cwbc, cbbc, dtb, alog, dskip_e, gain, heads, sb_width):
    b, s, _ = proj.shape
    l = SSD_CHUNK
    width = heads * SSD_HEAD_DIM
    bcw = 2 * SSD_GROUPS * SSD_STATE
    z_off = 3 * sb_width
    xs_off = z_off + width
    bc_off = xs_off + width
    dt_off = bc_off + bcw
    assert z_off % width == 0 and xs_off % width == 0 and bc_off % bcw == 0 and dt_off % LANE == 0
    const = lambda shape: pl.BlockSpec(shape, lambda bi, ci: (0, 0))
    kernel = functools.partial(_ssd_kernel, heads=heads)
    return pl.pallas_call(
        kernel,
        grid=(b, s // l),
        in_specs=[
            pl.BlockSpec((None, l, width), lambda bi, ci: (bi, ci, xs_off // width)),
            pl.BlockSpec((None, l, bcw), lambda bi, ci: (bi, ci, bc_off // bcw)),
            pl.BlockSpec((None, l, LANE), lambda bi, ci: (bi, ci, dt_off // LANE)),
            pl.BlockSpec((None, l, width), lambda bi, ci: (bi, ci, z_off // width)),
            const((SSD_CONV, width)), const((1, width)),
            const((SSD_CONV, bcw)), const((1, bcw)),
            const((1, LANE)), const((1, LANE)), const((1, width)), const((1, width)),
        ],
        out_specs=pl.BlockSpec((None, l, width), lambda bi, ci: (bi, ci, 0)),
        out_shape=jax.ShapeDtypeStruct((b, s, width), F32),
        scratch_shapes=[
            pltpu.VMEM((SUBLANE + l, width), F32),
            pltpu.VMEM((SUBLANE + l, bcw), F32),
            pltpu.VMEM((SSD_STATE, width), F32),
        ],
        compiler_params=_params("parallel", "arbitrary"),
        name="ssd",
    )(proj, proj, proj, proj, cwx, cbx, cwbc, cbbc, dtb, alog, dskip_e, gain)


def _out_proj_kernel(attn_ref, ssd_ref, x_ref, ga_ref, wa_ref, ws_ref, gf_ref, wq_ref, keys_ref,
                     h_ref, hn_ref, s_ref):
    a = attn_ref[...]
    inv = lax.rsqrt(jnp.mean(a * a, axis=-1, keepdims=True) + RMS_EPS)
    an = ((a * inv) * ga_ref[...]).astype(BF16)
    h = (x_ref[...]
         + jnp.dot(an, wa_ref[...], preferred_element_type=F32)
         + jnp.dot(ssd_ref[...].astype(BF16), ws_ref[...], preferred_element_type=F32))
    h_ref[...] = h
    inv_h = lax.rsqrt(jnp.mean(h * h, axis=-1, keepdims=True) + RMS_EPS)
    hn = ((h * inv_h) * gf_ref[...]).astype(BF16)
    hn_ref[...] = hn
    q = jnp.dot(hn, wq_ref[...], preferred_element_type=F32)
    half = keys_ref.shape[-1]
    for hc in range(keys_ref.shape[0]):
        s_ref[hc] = _nt_dot(keys_ref[hc], q[:, hc * half:(hc + 1) * half],
                            precision=lax.Precision.HIGHEST)


def _out_proj(attn2, ssd2, x2, ga, wa, ws, gf, wq, keys, tm):
    t, d = x2.shape
    wa_rows, ws_rows = wa.shape[0], ws.shape[0]
    n_hc, n_keys, half = keys.shape
    once = dict(pipeline_mode=pl.Buffered(1))
    return pl.pallas_call(
        _out_proj_kernel,
        grid=(t // tm,),
        in_specs=[
            pl.BlockSpec((tm, wa_rows), lambda i: (i, 0)),
            pl.BlockSpec((tm, ws_rows), lambda i: (i, 0)),
            pl.BlockSpec((tm, d), lambda i: (i, 0)),
            pl.BlockSpec((1, wa_rows), lambda i: (0, 0)),
            pl.BlockSpec((wa_rows, d), lambda i: (0, 0), **once),
            pl.BlockSpec((ws_rows, d), lambda i: (0, 0), **once),
            pl.BlockSpec((1, d), lambda i: (0, 0)),
            pl.BlockSpec((d, n_hc * half), lambda i: (0, 0), **once),
            pl.BlockSpec((n_hc, n_keys, half), lambda i: (0, 0, 0), **once),
        ],
        out_specs=[pl.BlockSpec((tm, d), lambda i: (i, 0)),
                   pl.BlockSpec((tm, d), lambda i: (i, 0)),
                   pl.BlockSpec((n_hc, n_keys, tm), lambda i: (0, 0, i))],
        out_shape=[jax.ShapeDtypeStruct((t, d), F32), jax.ShapeDtypeStruct((t, d), BF16),
                   jax.ShapeDtypeStruct((n_hc, n_keys, t), F32)],
        compiler_params=_params("parallel"),
        name="out_proj",
    )(attn2, ssd2, x2, ga, wa, ws, gf, wq, keys)


def _top_k_rows(scores, k, pos=None):
    work = scores
    rank = jnp.full(scores.shape, float(k), F32)
    tops = []
    for r in range(k):
        m = jnp.max(work, axis=0, keepdims=True)
        hit = work == m
        if pos is not None:
            first = jnp.min(jnp.where(hit, pos, float("inf")), axis=0, keepdims=True)
            hit = pos == first
        rank = jnp.where(hit, float(r), rank)
        work = jnp.where(hit, NEG_INF, work)
        tops.append(m)
    count = jnp.sum(jnp.where(rank < float(k), 1.0, 0.0), axis=0, keepdims=True)
    return tops, rank, count


def _row_iota(rows, cols):
    return lax.broadcasted_iota(jnp.int32, (rows, cols), 0).astype(F32)


def _peer_gate_factors(s0, s1, k, tie_safe):
    assert k == 2 * SUBLANE
    keys, tm = s0.shape
    pos = _row_iota(keys, tm) if tie_safe else None
    tops0, rank0, count0 = _top_k_rows(s0, k, pos)
    tops1, rank1, count1 = _top_k_rows(s1, k, pos)
    t0 = jnp.concatenate(tops0, axis=0)
    t1 = jnp.concatenate(tops1, axis=0)
    i8 = _row_iota(SUBLANE, tm)
    blocks = [tops0[0] + t1, tops0[1] + t1[:SUBLANE]]
    cpos = [_row_iota(k, tm), float(k) + i8]
    for a in range(2, SUBLANE):
        blocks.append(jnp.where(i8 < float(k // (a + 1)), tops0[a] + t1[:SUBLANE], NEG_INF))
        cpos.append(float(a * k) + i8)
    blocks.append(t0[SUBLANE:] + tops1[0])
    cpos.append((i8 + float(SUBLANE)) * float(k))
    cand = jnp.concatenate(blocks, axis=0)
    bests, crank, count2 = _top_k_rows(cand, k, jnp.concatenate(cpos, axis=0) if tie_safe else None)

    sel = jnp.where(crank < float(k), 1.0, 0.0)
    z = jnp.zeros_like(bests[0])
    for r in range(k):
        z = z + jnp.exp(bests[r] - bests[0])
    n0 = jnp.zeros_like(s0)
    row = 0
    for a in range(SUBLANE):
        rows_a = k if a == 0 else SUBLANE
        n_a = jnp.sum(sel[row:row + rows_a, :], axis=0, keepdims=True)
        n0 = jnp.where(rank0 == float(a), n_a, n0)
        row += rows_a
    tail = jnp.sum(sel[row:row + SUBLANE, :], axis=0, keepdims=True)
    prefix = jnp.where(rank0 < float(SUBLANE) + tail, 1.0, 0.0)
    n0 = jnp.where(rank0 >= float(SUBLANE), prefix, n0)
    w0 = (0.5 * jnp.exp(s0 - tops0[0])) / z
    w1 = jnp.exp(s1 - tops1[0])
    most = jnp.max(jnp.maximum(jnp.maximum(count0, count1), count2))
    return w0, n0, w1, rank1, most


def _peer_route_kernel(s_ref, w0_ref, n0_ref, w1_ref, r1_ref):
    k = PEER_TOPK

    def emit(tie_safe):
        most = None
        for g in range(s_ref.shape[-1] // LANE):
            lanes = pl.ds(g * LANE, LANE)
            w0, n0, w1, r1, most_g = _peer_gate_factors(s_ref[0, :, lanes], s_ref[1, :, lanes],
                                                        k, tie_safe)
            w0_ref[:, lanes] = w0
            n0_ref[:, lanes] = n0
            w1_ref[:, lanes] = w1.astype(BF16)
            r1_ref[:, lanes] = r1.astype(BF16)
            most = most_g if most is None else jnp.maximum(most, most_g)
        return most

    most = emit(tie_safe=False)

    @pl.when(most > float(k))
    def _():
        emit(tie_safe=True)


def _peer_route(scores, tm):
    n_hc, n_keys, t = scores.shape
    heads = n_hc // 2
    out32 = jax.ShapeDtypeStruct((heads, n_keys, t), F32)
    out16 = jax.ShapeDtypeStruct((heads, n_keys, t), BF16)
    ospec = pl.BlockSpec((None, n_keys, tm), lambda i, h: (h, 0, i))
    return pl.pallas_call(
        _peer_route_kernel,
        grid=(t // tm, heads),
        in_specs=[pl.BlockSpec((2, n_keys, tm), lambda i, h: (h, 0, i))],
        out_specs=[ospec, ospec, ospec, ospec],
        out_shape=[out32, out32, out16, out16],
        compiler_params=_params("parallel", "parallel"),
        name="peer_route",
    )(scores)


def _peer_experts_kernel(hn0_ref, hn1_ref, down_ref, upt_prev_ref, upt_ref,
                         w0a_ref, n0a_ref, w1a_ref, r1a_ref, w0b_ref, n0b_ref, w1b_ref, r1b_ref,
                         oa_ref, ob_ref, pa_ref, pb_ref, *, heads, n_keys):
    g = pl.program_id(1)
    n_steps = pl.num_programs(1) - 1
    half = pa_ref.shape[1]

    @pl.when(g == 0)
    def _():
        oa_ref[...] = jnp.zeros_like(oa_ref)
        ob_ref[...] = jnp.zeros_like(ob_ref)
        pb_ref[...] = jnp.zeros_like(pb_ref)

    blocks = ((hn0_ref, w0a_ref, n0a_ref, w1a_ref, r1a_ref, oa_ref, pa_ref.at[0], pb_ref.at[0]),
              (hn1_ref, w0b_ref, n0b_ref, w1b_ref, r1b_ref, ob_ref, pa_ref.at[1], pb_ref.at[1]))

    @pl.when(g < n_steps)
    def _():
        _peer_token_block(*blocks[0], down_ref, upt_prev_ref, upt_ref, heads, n_keys, half, False)

    @pl.when(g <= n_steps - 1)
    def _():
        _peer_token_block(*blocks[1], down_ref, upt_prev_ref, upt_ref, heads, n_keys, half, False)

    @pl.when(g == n_steps)
    def _():
        for blk in blocks:
            _peer_token_block(*blk, down_ref, upt_prev_ref, upt_ref, heads, n_keys, half, True)


def _peer_token_block(hn_ref, w0_ref, n0_ref, w1_ref, r1_ref, acc_ref, pa_ref, pb_ref,
                      down_ref, upt_prev_ref, upt_ref, heads, n_keys, half, last):
    tb = hn_ref.shape[0]
    rows_per_chunk = half // n_keys

    def activate(c, p_ref):
        a = _nt_dot(down_ref[pl.ds(c * half, half), :], hn_ref[...])
        act = (a * (1.0 + lax.erf(a * (1.0 / math.sqrt(2.0))))).astype(BF16)
        for rr in range(rows_per_chunk):
            r = c * rows_per_chunk + rr
            for tc in range(tb // LANE):
                lanes = slice(tc * LANE, (tc + 1) * LANE)
                gate = jnp.zeros((n_keys // BF16_ROWS, BF16_ROWS, LANE), BF16)
                for h in range(heads):
                    n_row = jnp.broadcast_to(n0_ref[h, pl.ds(r, 1), lanes], (BF16_ROWS, LANE))
                    w_row = jnp.broadcast_to(w0_ref[h, pl.ds(r, 1), lanes], (BF16_ROWS, LANE))
                    n_row = n_row.astype(BF16)[None]
                    w_row = w_row.astype(BF16)[None]
                    live = jnp.maximum(n_row - r1_ref[h, :, :, lanes], 0.0)
                    gate = gate + w_row * jnp.minimum(live, w1_ref[h, :, :, lanes])
                rows = slice(rr * n_keys, (rr + 1) * n_keys)
                p_ref[rows, lanes] = gate.reshape(n_keys, LANE) * act[rows, lanes]

    def up_project(w_ref, p_ref):
        for s in range(0, acc_ref.shape[0], MXU_ACC_ROWS):
            rows = pl.ds(s, MXU_ACC_ROWS)
            acc_ref[rows, :] += jnp.dot(w_ref[rows, :], p_ref[...], preferred_element_type=F32)

    if last:
        up_project(upt_prev_ref, pb_ref)
    else:
        up_project(upt_prev_ref, pb_ref)
        activate(0, pa_ref)
        activate(1, pb_ref)
        up_project(upt_ref, pa_ref)


def _peer_experts(hn, down, up, w0, n0, w1, r1, tb, rows_per_step):
    t, d = hn.shape
    heads, n_keys, _ = w0.shape
    assert rows_per_step % 2 == 0 and t % (2 * tb) == 0
    half = rows_per_step * n_keys // 2
    n_chunks = down.shape[0] // half
    n_steps = n_chunks // 2
    upt = up.reshape(n_chunks, half, d).transpose(0, 2, 1)
    w1 = w1.reshape(heads, n_keys // BF16_ROWS, BF16_ROWS, t)
    r1 = r1.reshape(heads, n_keys // BF16_ROWS, BF16_ROWS, t)
    once = dict(pipeline_mode=pl.Buffered(1))

    def token_specs(k):
        fspec = pl.BlockSpec((heads, n_keys // BF16_ROWS, BF16_ROWS, tb),
                             lambda i, g: (0, 0, 0, 2 * i + k))
        rspec = pl.BlockSpec((heads, rows_per_step, tb),
                             lambda i, g: (0, jnp.minimum(g, n_steps - 1), 2 * i + k))
        return [rspec, rspec, fspec, fspec]

    kernel = functools.partial(_peer_experts_kernel, heads=heads, n_keys=n_keys)
    out = jax.ShapeDtypeStruct((d, t // 2), F32)
    return pl.pallas_call(
        kernel,
        grid=(t // (2 * tb), n_steps + 1),
        in_specs=[
            pl.BlockSpec((tb, d), lambda i, g: (2 * i, 0), **once),
            pl.BlockSpec((tb, d), lambda i, g: (2 * i + 1, 0), **once),
            pl.BlockSpec((2 * half, d), lambda i, g: (jnp.minimum(g, n_steps - 1), 0)),
            pl.BlockSpec((None, d, half), lambda i, g: (jnp.maximum(2 * g - 1, 0), 0, 0)),
            pl.BlockSpec((None, d, half), lambda i, g: (jnp.minimum(2 * g, n_chunks - 1), 0, 0)),
            *token_specs(0), *token_specs(1),
        ],
        out_specs=[pl.BlockSpec((d, tb), lambda i, g: (0, i)),
                   pl.BlockSpec((d, tb), lambda i, g: (0, i))],
        out_shape=[out, out],
        scratch_shapes=[
            pltpu.VMEM((2, half, tb), BF16),
            pltpu.VMEM((2, half, tb), BF16),
        ],
        compiler_params=_params("parallel", "arbitrary"),
        name="peer_experts",
    )(hn, hn, down, upt, upt, w0, n0, w1, r1, w0, n0, w1, r1)


def _residual_kernel(h_ref, pe_ref, po_ref, y_ref):
    tb = pe_ref.shape[1]
    y_ref[pl.ds(0, tb), :] = h_ref[pl.ds(0, tb), :] + pe_ref[...].T
    y_ref[pl.ds(tb, tb), :] = h_ref[pl.ds(tb, tb), :] + po_ref[...].T


def _residual(h2, peer_even, peer_odd, tb):
    t, d = h2.shape
    return pl.pallas_call(
        _residual_kernel,
        grid=(t // (2 * tb),),
        in_specs=[pl.BlockSpec((2 * tb, d), lambda i: (i, 0)),
                  pl.BlockSpec((d, tb), lambda i: (0, i)),
                  pl.BlockSpec((d, tb), lambda i: (0, i))],
        out_specs=pl.BlockSpec((2 * tb, d), lambda i: (i, 0)),
        out_shape=jax.ShapeDtypeStruct((t, d), F32),
        compiler_params=_params("parallel"),
        name="peer_residual",
    )(h2, peer_even, peer_odd)


def _largest_tile(total, limit, unit):
    best = unit
    for cand in range(unit, min(total, limit) + 1, unit):
        if total % cand == 0:
            best = cand
    return best


def _pad_cols(a, cols):
    return jnp.pad(a, ((0, 0), (0, cols - a.shape[1])))


def kernel(x, attn_norm, w_in, sb_q_norm, sb_k_norm, conv_w, conv_b, dt_bias, a_log, d_skip,
           sb_out_norm, ssd_out_norm, w_out, ffn_norm, peer_query, peer_sub_keys, peer_down, peer_up):
    b, s, d = x.shape
    t = b * s
    depth = w_in.shape[0]
    sb_width = sb_out_norm.shape[-1]
    sb_heads = sb_width // SB_HEAD_DIM
    ssd_width = ssd_out_norm.shape[-1]
    ssd_heads = dt_bias.shape[-1]
    assert ssd_heads * SSD_HEAD_DIM == ssd_width and ssd_heads <= LANE
    assert t % MXU_ACC_ROWS == 0 and d % MXU_ACC_ROWS == 0
    peer_heads, _, n_keys, half = peer_sub_keys.shape[1:]
    n_in = w_in.shape[-1]
    n_in_pad = -(-n_in // LANE) * LANE

    tm_in = _largest_tile(t, 2 * MXU_ACC_ROWS, MXU_ACC_ROWS)
    tn_in = _largest_tile(n_in_pad, 1152, LANE)
    tq = _largest_tile(s, 256, LANE)
    sb_chains = _largest_tile(s // tq, 4, 1)
    tm_out = _largest_tile(t, 256, SUBLANE)
    tm_route = _largest_tile(t, 256, LANE)
    tb_peer = _largest_tile(t, 512, LANE)
    rows_per_step = _largest_tile(n_keys, SUBLANE, SUBLANE)

    h2 = x.reshape(t, d)
    for layer in range(depth):
        w_in_l = _pad_cols(w_in[layer], n_in_pad).astype(BF16)
        proj = _in_proj(h2, attn_norm[layer][None, :], w_in_l, tm_in, tn_in).reshape(b, s, n_in_pad)

        attn = _sb_attn(proj, sb_q_norm[layer][None, :], sb_k_norm[layer][None, :], sb_heads, tq,
                        sb_chains)

        cw = conv_w[layer][:, 0, :]
        cb = conv_b[layer][None, :]
        ssd = _ssd(proj, cw[:, :ssd_width], cb[:, :ssd_width], cw[:, ssd_width:], cb[:, ssd_width:],
                   _pad_cols(dt_bias[layer][None, :], LANE), _pad_cols(a_log[layer][None, :], LANE),
                   jnp.repeat(d_skip[layer], SSD_HEAD_DIM)[None, :], ssd_out_norm[layer][None, :],
                   ssd_heads, sb_width)

        w_out_l = w_out[layer].astype(BF16)
        h2, hn, scores = _out_proj(attn.reshape(t, sb_width), ssd.reshape(t, ssd_width), h2,
                                   sb_out_norm[layer][None, :], w_out_l[:sb_width], w_out_l[sb_width:],
                                   ffn_norm[layer][None, :], peer_query[layer].astype(BF16),
                                   peer_sub_keys[layer].reshape(2 * peer_heads, n_keys, half), tm_out)

        w0, n0, w1, r1 = _peer_route(scores, tm_route)
        peer_even, peer_odd = _peer_experts(hn, peer_down[layer].astype(BF16),
                                            peer_up[layer].astype(BF16),
                                            w0, n0, w1, r1, tb_peer, rows_per_step)
        h2 = _residual(h2, peer_even, peer_odd, tb_peer)
    return h2.reshape(b, s, d)
```
